```python
import math
import jax
import jax.numpy as jnp
from jax import lax
import numpy as np

D_MODEL = 2048
BATCH = 4
SEQ = 2048
DEPTH = 4
DEC_BATCH = 128
DEC_SEQ = 4
PAST_LEN = 16384
PAGE_SIZE = 128

N_MIXERS = 2
HEAD_SIZE = 64
N_HEADS = D_MODEL // HEAD_SIZE
DECAY_LORA = max(32, int(round(D_MODEL ** 0.5 * 1.8 / 32)) * 32)
AAA_LORA = max(32, int(round(D_MODEL ** 0.5 * 1.8 / 32)) * 32)
MV_LORA = max(32, int(round(D_MODEL ** 0.5 * 1.3 / 32)) * 32)
GATE_LORA = max(32, int(round(D_MODEL ** 0.8 * 0.6 / 32)) * 32)
GN_EPS = 64e-5
LN_EPS = 1e-5
POOL_WINDOWS = (2, 4, 8, 16)
POOL_GROUPS = len(POOL_WINDOWS)
POOL_GROUP_WIDTH = D_MODEL // POOL_GROUPS
POOL_BUF = max(POOL_WINDOWS) - 1
D_FF = int(math.ceil(8 * D_MODEL / 3 / 256)) * 256
ALPHA = (2 * DEPTH) ** 0.25
BETA = (8 * DEPTH) ** -0.25
N_RWKV = (DEPTH + N_MIXERS - 1) // N_MIXERS
N_POOL = DEPTH // N_MIXERS

kernel_name = "rwkv7_pool_interleaved_deepnorm_step"


def _layer_norm(x, g, b):
    xf = x.astype(jnp.float32)
    mu = jnp.mean(xf, axis=-1, keepdims=True)
    var = jnp.mean(jnp.square(xf - mu), axis=-1, keepdims=True)
    return ((xf - mu) * lax.rsqrt(var + LN_EPS) * g.astype(jnp.float32) + b.astype(jnp.float32)).astype(x.dtype)


def _wkv_step(S, inp):
    r_t, w_t, k_t, v_t, a_t, b_t = inp
    Sa = jnp.einsum('bhvk,bhk->bhv', S, a_t)
    S = S * w_t[:, :, None, :] + Sa[..., None] * b_t[:, :, None, :] + v_t[..., None] * k_t[:, :, None, :]
    y = jnp.einsum('bhvk,bhk->bhv', S, r_t)
    return S, y


def _rwkv7_mix(x, shift_prev, wkv_prev, v_first, mu, w_r, w_k, w_v, w_o, w0, w1, w2,
               a0, a1, a2, g1, g2, k_k, k_a, r_k, gn_g, gn_b, vres):
    B, T, D = x.shape
    H, N = N_HEADS, HEAD_SIZE
    f32 = jnp.float32
    x_prev = jnp.concatenate([shift_prev[:, None, :].astype(x.dtype), x[:, :-1]], axis=1)
    xx = x_prev - x
    xr = x + xx * mu[0]
    xw = x + xx * mu[1]
    xk = x + xx * mu[2]
    xv = x + xx * mu[3]
    xa = x + xx * mu[4]
    xg = x + xx * mu[5]
    r = xr @ w_r
    w_log = -jax.nn.softplus(-(w0 + jnp.tanh(xw @ w1) @ w2).astype(f32)) - 0.5
    decay = jnp.exp(-jnp.exp(w_log))
    k = xk @ w_k
    v = xv @ w_v
    if vres is None:
        v_first = v
    else:
        v0, v1, v2 = vres
        v = v + (v_first - v) * jax.nn.sigmoid(v0 + (xv @ v1) @ v2)
    a = jax.nn.sigmoid(a0 + (xa @ a1) @ a2)
    g = jax.nn.sigmoid(xg @ g1) @ g2
    kk = (k * k_k).astype(f32).reshape(B, T, H, N)
    kk = kk / jnp.maximum(jnp.sqrt(jnp.sum(kk * kk, axis=-1, keepdims=True)), 1e-12)
    k = k * (1 + (a - 1) * k_a)

    def heads(z):
        return z.astype(f32).reshape(B, T, H, N)

    r_h, k_h, v_h, a_h = heads(r), heads(k), heads(v), heads(a)
    xs = tuple(jnp.swapaxes(z, 0, 1) for z in (r_h, heads(decay), k_h, v_h, -kk, kk * a_h))
    S_last, ys = lax.scan(_wkv_step, wkv_prev.astype(f32), xs)
    y = jnp.swapaxes(ys, 0, 1)
    ym = jnp.mean(y, axis=-1, keepdims=True)
    yv = jnp.mean(jnp.square(y - ym), axis=-1, keepdims=True)
    y = ((y - ym) * lax.rsqrt(yv + GN_EPS)).reshape(B, T, D) * gn_g.astype(f32) + gn_b.astype(f32)
    bonus = jnp.sum(r_h * k_h * r_k.astype(f32), axis=-1, keepdims=True) * v_h
    y = (y + bonus.reshape(B, T, D)).astype(x.dtype)
    out = (y * g) @ w_o
    return out, x[:, -1], S_last, v_first


def _pool_mix(x, past, start_pos, w_grp, scale):
    B, T, D = x.shape
    f32 = jnp.float32
    ext = jnp.concatenate([past.astype(x.dtype), x], axis=1)
    ext32 = ext.astype(f32)
    cs = jnp.concatenate([jnp.zeros((B, 1, D), f32), jnp.cumsum(ext32, axis=1)], axis=1)
    end = cs[:, POOL_BUF + 1:]
    pos = start_pos + jnp.arange(T)
    outs = []
    for gi, win in enumerate(POOL_WINDOWS):
        sl = slice(gi * POOL_GROUP_WIDTH, (gi + 1) * POOL_GROUP_WIDTH)
        lo = POOL_BUF + 1 - win
        start = cs[:, lo:lo + T, sl]
        cnt = jnp.minimum(win, pos + 1).astype(f32)[None, :, None]
        pooled = (end[..., sl] - start) / cnt
        outs.append(pooled - ext32[:, POOL_BUF:, sl])
    d = jnp.stack(outs, axis=2).astype(x.dtype)
    y = jnp.einsum('btgi,gio->btgo', d, w_grp).reshape(B, T, D)
    return y * scale, ext[:, -POOL_BUF:]


def _swiglu(x, w_in, w_down):
    h = x @ w_in
    gate, up = h[..., :D_FF], h[..., D_FF:]
    return (jax.nn.silu(gate) * up) @ w_down


def _trunk(x, wkv_st, shift_st, pool_st, start_pos, params):
    (ln_g, ln_b, rw_mu, rw_wr, rw_wk, rw_wv, rw_wo, rw_w0, rw_w1, rw_w2,
     rw_a0, rw_a1, rw_a2, rw_v0, rw_v1, rw_v2, rw_g1, rw_g2, rw_kk, rw_ka, rw_rk,
     rw_gn_g, rw_gn_b, pool_w, pool_scale, ffn_w_in, ffn_w_down) = params
    v_first = None
    new_wkv, new_shift, new_pool = [], [], []
    for i in range(DEPTH):
        j = i // N_MIXERS
        if i % N_MIXERS == 0:
            vres = None if j == 0 else (rw_v0[j - 1], rw_v1[j - 1], rw_v2[j - 1])
            h, sh, S, v_first = _rwkv7_mix(
                x, shift_st[j], wkv_st[j], v_first, rw_mu[j], rw_wr[j], rw_wk[j], rw_wv[j], rw_wo[j],
                rw_w0[j], rw_w1[j], rw_w2[j], rw_a0[j], rw_a1[j], rw_a2[j], rw_g1[j], rw_g2[j],
                rw_kk[j], rw_ka[j], rw_rk[j], rw_gn_g[j], rw_gn_b[j], vres)
            new_wkv.append(S)
            new_shift.append(sh)
        else:
            h, buf = _pool_mix(x, pool_st[j], start_pos, pool_w[j], pool_scale[j])
            new_pool.append(buf)
        x = _layer_norm(ALPHA * x + h, ln_g[i, 0], ln_b[i, 0])
        x = _layer_norm(ALPHA * x + _swiglu(x, ffn_w_in[i], ffn_w_down[i]), ln_g[i, 1], ln_b[i, 1])
    return x, jnp.stack(new_wkv), jnp.stack(new_shift), jnp.stack(new_pool)


def setup_inputs(seed: int = 0) -> dict:
    key = jax.random.key(seed)
    ks = iter(jax.random.split(key, 40))
    nrm = lambda shape, s: jax.random.normal(next(ks), shape, jnp.float32) * s
    D, H, N = D_MODEL, N_HEADS, HEAD_SIZE
    G, W = POOL_GROUPS, POOL_GROUP_WIDTH
    nv = max(N_RWKV - 1, 1)
    return {
        "x_prompt": nrm((BATCH, SEQ, D), 1.0),
        "x_sample": nrm((DEC_BATCH, DEC_SEQ, D), 1.0),
        "state_wkv": nrm((N_RWKV, DEC_BATCH, H, N, N), 0.3),
        "state_shift": nrm((N_RWKV, DEC_BATCH, D), 1.0),
        "state_pool": nrm((N_POOL, DEC_BATCH, POOL_BUF, D), 1.0),
        "ln_g": 1.0 + nrm((DEPTH, 2, D), 0.05),
        "ln_b": nrm((DEPTH, 2, D), 0.02),
        "rw_mu": jax.random.uniform(next(ks), (N_RWKV, 6, D), jnp.float32),
        "rw_wr": nrm((N_RWKV, D, D), D ** -0.5),
        "rw_wk": nrm((N_RWKV, D, D), D ** -0.5),
        "rw_wv": nrm((N_RWKV, D, D), D ** -0.5),
        "rw_wo": nrm((N_RWKV, D, D), D ** -0.5 * BETA),
        "rw_w0": jax.random.uniform(next(ks), (N_RWKV, D), jnp.float32, -3.0, 1.0),
        "rw_w1": nrm((N_RWKV, D, DECAY_LORA), D ** -0.5),
        "rw_w2": nrm((N_RWKV, DECAY_LORA, D), 0.1 * DECAY_LORA ** -0.5),
        "rw_a0": nrm((N_RWKV, D), 0.1),
        "rw_a1": nrm((N_RWKV, D, AAA_LORA), D ** -0.5),
        "rw_a2": nrm((N_RWKV, AAA_LORA, D), 0.3 * AAA_LORA ** -0.5),
        "rw_v0": nrm((nv, D), 0.1),
        "rw_v1": nrm((nv, D, MV_LORA), D ** -0.5),
        "rw_v2": nrm((nv, MV_LORA, D), 0.3 * MV_LORA ** -0.5),
        "rw_g1": nrm((N_RWKV, D, GATE_LORA), D ** -0.5),
        "rw_g2": nrm((N_RWKV, GATE_LORA, D), GATE_LORA ** -0.5),
        "rw_kk": 0.85 + nrm((N_RWKV, D), 0.05),
        "rw_ka": 1.0 + nrm((N_RWKV, D), 0.05),
        "rw_rk": nrm((N_RWKV, H, N), 0.1),
        "rw_gn_g": 1.0 + nrm((N_RWKV, D), 0.05),
        "rw_gn_b": nrm((N_RWKV, D), 0.02),
        "pool_w": nrm((N_POOL, G, W, W), W ** -0.5 * BETA),
        "pool_scale": 1.0 + nrm((N_POOL, D), 0.1),
        "ffn_w_in": nrm((DEPTH, D, 2 * D_FF), D ** -0.5),
        "ffn_w_down": nrm((DEPTH, D_FF, D), D_FF ** -0.5 * BETA),
    }


def reference(x_prompt, x_sample, state_wkv, state_shift, state_pool,
              ln_g, ln_b, rw_mu, rw_wr, rw_wk, rw_wv, rw_wo, rw_w0, rw_w1, rw_w2,
              rw_a0, rw_a1, rw_a2, rw_v0, rw_v1, rw_v2, rw_g1, rw_g2, rw_kk, rw_ka, rw_rk,
              rw_gn_g, rw_gn_b, pool_w, pool_scale, ffn_w_in, ffn_w_down):
    params = (ln_g, ln_b, rw_mu, rw_wr, rw_wk, rw_wv, rw_wo, rw_w0, rw_w1, rw_w2,
              rw_a0, rw_a1, rw_a2, rw_v0, rw_v1, rw_v2, rw_g1, rw_g2, rw_kk, rw_ka, rw_rk,
              rw_gn_g, rw_gn_b, pool_w, pool_scale, ffn_w_in, ffn_w_down)
    Bp = x_prompt.shape[0]
    wkv0 = jnp.zeros((N_RWKV, Bp, N_HEADS, HEAD_SIZE, HEAD_SIZE), jnp.float32)
    shift0 = jnp.zeros((N_RWKV, Bp, D_MODEL), x_prompt.dtype)
    pool0 = jnp.zeros((N_POOL, Bp, POOL_BUF, D_MODEL), x_prompt.dtype)
    y_prompt, wkv_p, shift_p, pool_p = _trunk(x_prompt, wkv0, shift0, pool0, 0, params)
    y_sample, wkv_s, shift_s, pool_s = _trunk(x_sample, state_wkv, state_shift, state_pool, PAST_LEN, params)
    return (y_prompt, y_sample, wkv_p, shift_p, pool_p, wkv_s, shift_s, pool_s)
```

```python
import functools
import math

import jax
import jax.numpy as jnp
from jax import lax
from jax.experimental import pallas as pl
from jax.experimental.pallas import tpu as pltpu

F32 = jnp.float32
BF16 = jnp.bfloat16

HEAD = 64
PAIR = 2 * HEAD
CHUNK = 64
SEQ_PER_CHUNK = 16
GN_EPS = 64e-5
LN_EPS = 1e-5
POOL_WINDOWS = (2, 4, 8, 16)
POOL_BUF = 15
HALO = 16
LORA_PAD = 128

TM = 512
TM_MIX = 256
TN = 512
TF = 512
WKV_ROWS = 512
WKV_PAIRS = 4
VMEM_LIMIT = 56 * 1024 * 1024


def _cparams(*sem):
    return pltpu.CompilerParams(dimension_semantics=sem, vmem_limit_bytes=VMEM_LIMIT)


def _mm(a, b):
    return jnp.dot(a, b, preferred_element_type=F32)


def _mm_nt(a, b):
    return lax.dot_general(a, b, (((1,), (1,)), ((), ())), preferred_element_type=F32)


def _mm_tn(a, b):
    return lax.dot_general(a, b, (((0,), (0,)), ((), ())), preferred_element_type=F32)


def _sigmoid(x):
    return 1.0 / (1.0 + jnp.exp(-x))


def _layer_norm(v, g, b):
    mu = jnp.mean(v, axis=-1, keepdims=True)
    c = v - mu
    var = jnp.mean(c * c, axis=-1, keepdims=True)
    return c * lax.rsqrt(var + LN_EPS) * g + b


def _split3(x):
    h1 = x.astype(BF16)
    r1 = x - h1.astype(F32)
    h2 = r1.astype(BF16)
    h3 = (r1 - h2.astype(F32)).astype(BF16)
    return h1, h2, h3


def _mix_kernel(has_vres, *refs):
    if has_vres:
        (x_ref, xp_ref, mu_ref, w0_ref, a0_ref, v0_ref, w1_ref, w2_ref, a1_ref, a2_ref, v1_ref, v2_ref,
         g1_ref, g2_ref, xr_ref, xk_ref, xv_ref, lw_ref, ag_ref, g_ref, vg_ref) = refs
    else:
        (x_ref, xp_ref, mu_ref, w0_ref, a0_ref, w1_ref, w2_ref, a1_ref, a2_ref,
         g1_ref, g2_ref, xr_ref, xk_ref, xv_ref, lw_ref, ag_ref, g_ref) = refs
    x = x_ref[...]
    xx = xp_ref[...] - x

    def mix(i):
        return (x + xx * mu_ref[i:i + 1, :]).astype(BF16)

    xr_ref[...] = mix(0)
    xk_ref[...] = mix(2)
    xv = mix(3)
    xv_ref[...] = xv

    u = w0_ref[...] + _mm(jnp.tanh(_mm(mix(1), w1_ref[...])).astype(BF16), w2_ref[...])
    z = -u
    softplus = jnp.maximum(z, 0.0) + jnp.log(1.0 + jnp.exp(-jnp.abs(z)))
    lw_ref[...] = -jnp.exp(-softplus - 0.5)
    ag_ref[...] = _sigmoid(a0_ref[...] + _mm(_mm(mix(4), a1_ref[...]).astype(BF16), a2_ref[...]))
    g_ref[...] = _mm(_sigmoid(_mm(mix(5), g1_ref[...])).astype(BF16), g2_ref[...])
    if has_vres:
        vg_ref[...] = _sigmoid(v0_ref[...] + _mm(_mm(xv, v1_ref[...]).astype(BF16), v2_ref[...]))


def _rwkv_mix(x, xprev, mu, w0, a0, v0, w1, w2, a1, a2, v1, v2, g1, g2):
    m, d = x.shape
    has_vres = v0 is not None
    tile = pl.BlockSpec((TM_MIX, d), lambda i: (i, 0))

    def full(a):
        return pl.BlockSpec(a.shape, lambda i: (0,) * a.ndim)

    if has_vres:
        args = (x, xprev, mu, w0, a0, v0, w1, w2, a1, a2, v1, v2, g1, g2)
    else:
        args = (x, xprev, mu, w0, a0, w1, w2, a1, a2, g1, g2)
    n_f32_out = 4 if has_vres else 3
    out_shape = [jax.ShapeDtypeStruct((m, d), BF16)] * 3 + [jax.ShapeDtypeStruct((m, d), F32)] * n_f32_out
    return pl.pallas_call(
        functools.partial(_mix_kernel, has_vres),
        grid=(m // TM_MIX,),
        in_specs=[tile, tile] + [full(a) for a in args[2:]],
        out_specs=[tile] * len(out_shape),
        out_shape=out_shape,
        compiler_params=_cparams("parallel"),
        name="rwkv_mix",
    )(*args)


def _rkv_kernel(xr_ref, xk_ref, xv_ref, wr_ref, wk_ref, wv_ref, r_ref, k_ref, v_ref):
    r_ref[...] = _mm(xr_ref[...], wr_ref[...])
    k_ref[...] = _mm(xk_ref[...], wk_ref[...])
    v_ref[...] = _mm(xv_ref[...], wv_ref[...])


def _rkv_proj(xr, xk, xv, wr, wk, wv):
    m, d = xr.shape
    xs = pl.BlockSpec((TM, d), lambda n, i: (i, 0))
    ws = pl.BlockSpec((d, TN), lambda n, i: (0, n))
    os = pl.BlockSpec((TM, TN), lambda n, i: (i, n))
    return pl.pallas_call(
        _rkv_kernel,
        grid=(d // TN, m // TM),
        in_specs=[xs, xs, xs, ws, ws, ws],
        out_specs=[os, os, os],
        out_shape=[jax.ShapeDtypeStruct((m, d), F32)] * 3,
        compiler_params=_cparams("parallel", "arbitrary"),
        name="rkv_proj",
    )(xr, xk, xv, wr, wk, wv)


class _ChunkMasks:
    def __init__(self, sample):
        row = lax.broadcasted_iota(jnp.int32, (CHUNK, PAIR), 0)
        lane = lax.broadcasted_iota(jnp.int32, (CHUNK, PAIR), 1)
        col = lane & (CHUNK - 1)
        r2 = lax.broadcasted_iota(jnp.int32, (CHUNK, CHUNK), 0)
        c2 = lax.broadcasted_iota(jnp.int32, (CHUNK, CHUNK), 1)
        if sample:
            shift = int(math.log2(SEQ_PER_CHUNK))
            seq_mask = SEQ_PER_CHUNK - 1
            same = (row & seq_mask) == (col & seq_mask)
            self.strict = same & ((col >> shift) < (row >> shift))
            self.incl = same & ((col >> shift) <= (row >> shift))
            tri = ((r2 & seq_mask) == (c2 & seq_mask)) & ((c2 >> shift) <= (r2 >> shift))
            self.doublings = 1
        else:
            self.strict = col < row
            self.incl = col <= row
            tri = c2 <= r2
            self.doublings = 5
        self.sample = sample
        self.eye = jnp.where(row == col, 1.0, 0.0).astype(F32)
        self.tri = jnp.where(tri, 1.0, 0.0).astype(BF16)
        self.head0 = lane < HEAD
        r128 = lax.broadcasted_iota(jnp.int32, (PAIR, PAIR), 0)
        c128 = lax.broadcasted_iota(jnp.int32, (PAIR, PAIR), 1)
        self.same_head = (r128 < HEAD) == (c128 < HEAD)
        self.diag128 = r128 == c128


def _blk(y, cm):
    return jnp.concatenate([jnp.where(cm.head0, y, 0.0), jnp.where(cm.head0, 0.0, y)], axis=0).astype(BF16)


def _headsum(x, cm):
    s0 = jnp.sum(jnp.where(cm.head0, x, 0.0), axis=1, keepdims=True)
    s1 = jnp.sum(jnp.where(cm.head0, 0.0, x), axis=1, keepdims=True)
    return jnp.where(cm.head0, s0, s1)


def _wkv_pre(kraw, v, ag, vf, vg, kkw, ka, cm):
    kkv = kraw * kkw
    norm = jnp.maximum(jnp.sqrt(_headsum(kkv * kkv, cm)), 1e-12)
    kkn = kkv / norm
    kmod = kraw * (1.0 + (ag - 1.0) * ka)
    if vf is not None:
        v = v + (vf - v) * vg
    return kmod, v, -kkn, kkn * ag


def _wkv_post(y, r, kmod, v, g, rk, gng, gnb, cm):
    inv_n = 1.0 / HEAD
    ym = _headsum(y, cm) * inv_n
    yc = y - ym
    yv = _headsum(yc * yc, cm) * inv_n
    yn = yc * lax.rsqrt(yv + GN_EPS) * gng + gnb
    bonus = _headsum(r * kmod * rk, cm) * v
    return ((yn + bonus) * g).astype(BF16)


def _chunk_prep(r, lw, k, v, a, b, cm):
    cum = sum(_mm(cm.tri, h) for h in _split3(lw))
    if cm.sample:
        last = cum[CHUNK - SEQ_PER_CHUNK:, :]
        cl = jnp.concatenate([last] * (CHUNK // SEQ_PER_CHUNK), axis=0)
    else:
        cl = jnp.broadcast_to(cum[CHUNK - 1:CHUNK, :], cum.shape)
    w_incl = jnp.exp(cum)
    w_excl = jnp.exp(cum - lw)
    w_inv = jnp.exp(-cum)
    w_last = jnp.exp(cl)
    rt = r * w_incl
    at = a * w_excl
    bt = b * w_inv
    kt = k * w_inv
    x = jnp.concatenate([at, rt], axis=0).astype(BF16)
    ab_rb = _mm_nt(x, _blk(bt, cm))
    ak_rk = _mm_nt(x, _blk(kt, cm))
    nab = jnp.where(cm.strict, ab_rb[:CHUNK], 0.0)
    nak = jnp.where(cm.strict, ak_rk[:CHUNK], 0.0)
    mrb = jnp.where(cm.incl, ab_rb[CHUNK:], 0.0).astype(BF16)
    mrk = jnp.where(cm.incl, ak_rk[CHUNK:], 0.0).astype(BF16)
    p = nab
    t = cm.eye + p
    for _ in range(cm.doublings):
        p = _mm(p.astype(BF16), _blk(p, cm))
        t = t + _mm(t.astype(BF16), _blk(p, cm))
    tb = t.astype(BF16)
    blk_v = _blk(v, cm)
    q = _mm(nak.astype(BF16), blk_v)
    ah = _mm(tb, _blk(at, cm))
    uh = _mm(tb, _blk(q, cm))
    rh = rt + _mm(mrb, _blk(ah, cm))
    yh = _mm(mrb, _blk(uh, cm)) + _mm(mrk, blk_v)
    return ah, uh, rh, yh, bt * w_last, kt * w_last, w_last


def _wkv_prompt_kernel(has_vres, n_chunks, n_pairs, *refs):
    if has_vres:
        (r_ref, k_ref, v_ref, lw_ref, ag_ref, g_ref, vf_ref, vg_ref,
         kkw_ref, ka_ref, rk_ref, gng_ref, gnb_ref, z_ref, s_ref, s_scr) = refs
    else:
        (r_ref, k_ref, v_ref, lw_ref, ag_ref, g_ref,
         kkw_ref, ka_ref, rk_ref, gng_ref, gnb_ref, z_ref, s_ref, s_scr) = refs
    step = pl.program_id(2)

    @pl.when(step == 0)
    def _():
        s_scr[...] = jnp.zeros_like(s_scr)

    cm = _ChunkMasks(sample=False)

    def chunk(c, carry):
        rows = pl.ds(pl.multiple_of(c * CHUNK, CHUNK), CHUNK)
        for p in range(n_pairs):
            lanes = slice(p * PAIR, (p + 1) * PAIR)
            r = r_ref[rows, lanes]
            vf = vf_ref[rows, lanes] if has_vres else None
            vg = vg_ref[rows, lanes] if has_vres else None
            kmod, v, a, b = _wkv_pre(k_ref[rows, lanes], v_ref[rows, lanes], ag_ref[rows, lanes], vf, vg,
                                     kkw_ref[:, lanes], ka_ref[:, lanes], cm)
            ah, uh, rh, yh, bh, kh, w_last = _chunk_prep(r, lw_ref[rows, lanes], kmod, v, a, b, cm)
            s = s_scr[p]
            s_hi = s.astype(BF16)
            s_lo = (s - s_hi.astype(F32)).astype(BF16)
            y = _mm_nt(rh.astype(BF16), s_hi) + yh
            bh16 = bh.astype(BF16)
            phi_t = jnp.where(cm.diag128, jnp.broadcast_to(w_last[:1, :], (PAIR, PAIR)), 0.0) + jnp.where(
                cm.same_head, _mm_tn(ah.astype(BF16), bh16), 0.0)
            psi_t = jnp.where(
                cm.same_head,
                _mm_tn(jnp.concatenate([uh, v], axis=0).astype(BF16),
                       jnp.concatenate([bh16, kh.astype(BF16)], axis=0)), 0.0)
            phi16 = phi_t.astype(BF16)
            s_scr[p] = _mm(s_hi, phi16) + _mm(s_lo, phi16) + psi_t
            z_ref[rows, lanes] = _wkv_post(y, r, kmod, v, g_ref[rows, lanes], rk_ref[:, lanes],
                                           gng_ref[:, lanes], gnb_ref[:, lanes], cm)
        return carry

    lax.fori_loop(0, n_chunks, chunk, 0)

    @pl.when(step == pl.num_programs(2) - 1)
    def _():
        s_ref[0] = s_scr[...]


def _wkv_prompt(tok, vres, chan, batch, seq, m_total):
    d = tok[0].shape[1]
    has_vres = vres is not None
    width = WKV_PAIRS * PAIR
    steps = seq // WKV_ROWS
    tile = pl.BlockSpec((WKV_ROWS, width), lambda b, p, s: (b * steps + s, p))
    row = pl.BlockSpec((1, width), lambda b, p, s: (0, p))
    args = tuple(tok) + (tuple(vres) if has_vres else ()) + tuple(chan)
    n_tok = len(tok) + (2 if has_vres else 0)
    return pl.pallas_call(
        functools.partial(_wkv_prompt_kernel, has_vres, WKV_ROWS // CHUNK, WKV_PAIRS),
        grid=(batch, d // width, steps),
        in_specs=[tile] * n_tok + [row] * len(chan),
        out_specs=[tile, pl.BlockSpec((1, WKV_PAIRS, PAIR, PAIR), lambda b, p, s: (b, p, 0, 0))],
        out_shape=[jax.ShapeDtypeStruct((m_total, d), BF16),
                   jax.ShapeDtypeStruct((batch, d // PAIR, PAIR, PAIR), F32)],
        scratch_shapes=[pltpu.VMEM((WKV_PAIRS, PAIR, PAIR), F32)],
        compiler_params=_cparams("parallel", "parallel", "arbitrary"),
        name="wkv_prompt",
    )(*args)


def _wkv_sample_kernel(has_vres, n_pairs, *refs):
    if has_vres:
        (z_in_ref, r_ref, k_ref, v_ref, lw_ref, ag_ref, g_ref, vf_ref, vg_ref,
         kkw_ref, ka_ref, rk_ref, gng_ref, gnb_ref, s_in_ref, z_ref, s_out_ref,
         x_scr, u_scr, bk_scr, gy_scr, wl_scr) = refs
    else:
        (z_in_ref, r_ref, k_ref, v_ref, lw_ref, ag_ref, g_ref,
         kkw_ref, ka_ref, rk_ref, gng_ref, gnb_ref, s_in_ref, z_ref, s_out_ref,
         x_scr, u_scr, bk_scr, gy_scr, wl_scr) = refs
    del z_in_ref
    cm = _ChunkMasks(sample=True)
    row8 = lax.broadcasted_iota(jnp.int32, (8, PAIR), 0)
    zeros_h = jnp.zeros((HEAD, HEAD), F32)
    steps = CHUNK // SEQ_PER_CHUNK

    for p in range(n_pairs):
        lanes = slice(p * PAIR, (p + 1) * PAIR)
        r = r_ref[:, lanes]
        vf = vf_ref[:, lanes] if has_vres else None
        vg = vg_ref[:, lanes] if has_vres else None
        kmod, v, a, b = _wkv_pre(k_ref[:, lanes], v_ref[:, lanes], ag_ref[:, lanes], vf, vg,
                                 kkw_ref[:, lanes], ka_ref[:, lanes], cm)
        ah, uh, rh, yh, bh, kh, w_last = _chunk_prep(r, lw_ref[:, lanes], kmod, v, a, b, cm)
        x_scr[0:CHUNK, :] = ah
        x_scr[CHUNK:, :] = rh
        u_scr[0:CHUNK, :] = uh
        u_scr[CHUNK:, :] = yh
        bk_scr[0:CHUNK, :] = bh
        bk_scr[CHUNK:, :] = kh
        gy_scr[CHUNK:, :] = v
        wl_scr[...] = w_last[:SEQ_PER_CHUNK, :]

        def one_sequence(i, carry):
            pick = pl.ds(i, 2 * steps, stride=SEQ_PER_CHUNK)
            s0 = s_in_ref[i, 2 * p]
            s1 = s_in_ref[i, 2 * p + 1]
            s = jnp.concatenate([jnp.concatenate([s0, zeros_h], axis=1),
                                 jnp.concatenate([zeros_h, s1], axis=1)], axis=0)
            uy = _mm_nt(x_scr[pick, :].astype(BF16), s.astype(BF16)) + u_scr[pick, :]
            uv = jnp.where(row8 < steps, uy, gy_scr[pick, :])
            gy_scr[pick, :] = uy
            upd = _mm_tn(uv.astype(BF16), bk_scr[pick, :].astype(BF16))
            s_new = s * wl_scr[pl.ds(i, 1), :] + jnp.where(cm.same_head, upd, 0.0)
            s_out_ref[i, 2 * p] = s_new[:HEAD, :HEAD]
            s_out_ref[i, 2 * p + 1] = s_new[HEAD:, HEAD:]
            return carry

        lax.fori_loop(0, SEQ_PER_CHUNK, one_sequence, 0)
        y = gy_scr[CHUNK:, :]
        z_ref[:, lanes] = _wkv_post(y, r, kmod, v, g_ref[:, lanes], rk_ref[:, lanes],
                                    gng_ref[:, lanes], gnb_ref[:, lanes], cm)


def _wkv_sample(z, tok, vres, chan, state, row0):
    m, d = z.shape
    nb, nh = state.shape[0], state.shape[1]
    has_vres = vres is not None
    width = WKV_PAIRS * PAIR
    blk0 = row0 // CHUNK
    tile = pl.BlockSpec((CHUNK, width), lambda i, p: (blk0 + i, p))
    row = pl.BlockSpec((1, width), lambda i, p: (0, p))
    st = pl.BlockSpec((SEQ_PER_CHUNK, 2 * WKV_PAIRS, HEAD, HEAD), lambda i, p: (i, p, 0, 0))
    args = (z,) + tuple(tok) + (tuple(vres) if has_vres else ()) + tuple(chan) + (state,)
    n_tok = len(tok) + (2 if has_vres else 0)
    return pl.pallas_call(
        functools.partial(_wkv_sample_kernel, has_vres, WKV_PAIRS),
        grid=(nb // SEQ_PER_CHUNK, d // width),
        in_specs=[pl.BlockSpec(memory_space=pl.ANY)] + [tile] * n_tok + [row] * len(chan) + [st],
        out_specs=[tile, st],
        out_shape=[jax.ShapeDtypeStruct((m, d), BF16), jax.ShapeDtypeStruct(state.shape, F32)],
        scratch_shapes=[pltpu.VMEM((2 * CHUNK, PAIR), F32)] * 4 + [pltpu.VMEM((SEQ_PER_CHUNK, PAIR), F32)],
        input_output_aliases={0: 0},
        compiler_params=_cparams("parallel", "parallel"),
        name="wkv_sample",
    )(*args)


def _out_ln_kernel(alpha, z_ref, x_ref, w_ref, g_ref, b_ref, o_ref, ob_ref):
    y = _layer_norm(alpha * x_ref[...] + _mm(z_ref[...], w_ref[...]), g_ref[...], b_ref[...])
    o_ref[...] = y
    ob_ref[...] = y.astype(BF16)


def _out_ln(alpha, z, x, w, g, b):
    m, d = x.shape
    tile = pl.BlockSpec((TM, d), lambda i: (i, 0))
    row = pl.BlockSpec((1, d), lambda i: (0, 0))
    return pl.pallas_call(
        functools.partial(_out_ln_kernel, alpha),
        grid=(m // TM,),
        in_specs=[tile, tile, pl.BlockSpec((d, d), lambda i: (0, 0)), row, row],
        out_specs=[tile, tile],
        out_shape=[jax.ShapeDtypeStruct((m, d), F32), jax.ShapeDtypeStruct((m, d), BF16)],
        compiler_params=_cparams("parallel"),
        name="out_ln",
    )(z, x, w, g, b)


def _ffn_kernel(alpha, xb_ref, x_ref, wg_ref, wu_ref, wd_ref, g_ref, b_ref, o_ref, ob_ref, acc_ref):
    f = pl.program_id(1)
    xb = xb_ref[...]
    gate = _mm(xb, wg_ref[...])
    up = _mm(xb, wu_ref[...])
    part = _mm((gate * _sigmoid(gate) * up).astype(BF16), wd_ref[...])

    @pl.when(f == 0)
    def _():
        acc_ref[...] = part

    @pl.when(f > 0)
    def _():
        acc_ref[...] += part

    @pl.when(f == pl.num_programs(1) - 1)
    def _():
        y = _layer_norm(alpha * x_ref[...] + acc_ref[...], g_ref[...], b_ref[...])
        o_ref[...] = y
        ob_ref[...] = y.astype(BF16)


def _ffn(alpha, xb, x, w_in, w_down, g, b):
    m, d = x.shape
    d_ff = w_down.shape[0]
    nf = d_ff // TF
    tile = pl.BlockSpec((TM, d), lambda i, f: (i, 0))
    row = pl.BlockSpec((1, d), lambda i, f: (0, 0))
    return pl.pallas_call(
        functools.partial(_ffn_kernel, alpha),
        grid=(m // TM, nf),
        in_specs=[tile, tile,
                  pl.BlockSpec((d, TF), lambda i, f: (0, f)),
                  pl.BlockSpec((d, TF), lambda i, f: (0, f + nf)),
                  pl.BlockSpec((TF, d), lambda i, f: (f, 0)),
                  row, row],
        out_specs=[tile, tile],
        out_shape=[jax.ShapeDtypeStruct((m, d), F32), jax.ShapeDtypeStruct((m, d), BF16)],
        scratch_shapes=[pltpu.VMEM((TM, d), F32)],
        compiler_params=_cparams("parallel", "arbitrary"),
        name="ffn",
    )(xb, x, w_in, w_in, w_down, g, b)


def _pool_prompt_kernel(alpha, tiles_per_seq, x_ref, halo_ref, w_ref, sc_ref, g_ref, b_ref, o_ref, ob_ref):
    tile_in_seq = pl.program_id(0) % tiles_per_seq
    x = x_ref[...]
    halo = jnp.where(tile_in_seq == 0, 0.0, halo_ref[...])
    gw = x.shape[1] // len(POOL_WINDOWS)
    pos = tile_in_seq * TM + lax.broadcasted_iota(jnp.int32, (TM, gw), 0)
    outs = []
    for gi, win in enumerate(POOL_WINDOWS):
        lanes = slice(gi * gw, (gi + 1) * gw)
        xg = x[:, lanes]
        s = jnp.concatenate([halo[:, lanes], xg], axis=0)
        span = 1
        while span < win:
            s = s[span:] + s[:-span]
            span *= 2
        first = HALO - (win - 1)
        cnt = jnp.minimum(win, pos + 1).astype(F32)
        dg = s[first:first + TM] / cnt - xg
        outs.append(_mm(dg.astype(BF16), w_ref[gi]))
    h = jnp.concatenate(outs, axis=1) * sc_ref[...]
    y = _layer_norm(alpha * x + h, g_ref[...], b_ref[...])
    o_ref[...] = y
    ob_ref[...] = y.astype(BF16)


def _pool_prompt(alpha, x, w, scale, g, b, m_prompt, seq):
    m, d = x.shape
    tile = pl.BlockSpec((TM, d), lambda i: (i, 0))
    row = pl.BlockSpec((1, d), lambda i: (0, 0))
    halo = pl.BlockSpec((HALO, d), lambda i: (jnp.maximum(i * (TM // HALO) - 1, 0), 0))
    return pl.pallas_call(
        functools.partial(_pool_prompt_kernel, alpha, seq // TM),
        grid=(m_prompt // TM,),
        in_specs=[tile, halo, pl.BlockSpec(w.shape, lambda i: (0, 0, 0)), row, row, row],
        out_specs=[tile, tile],
        out_shape=[jax.ShapeDtypeStruct((m, d), F32), jax.ShapeDtypeStruct((m, d), BF16)],
        compiler_params=_cparams("parallel"),
        name="pool_prompt",
    )(x, x, w, scale, g, b)


def _pool_sample_kernel(ext_ref, w_ref, sc_ref, h_ref):
    steps = ext_ref.shape[0] - POOL_BUF
    gi = pl.program_id(0)
    ds = []
    for t in range(steps):
        cur = ext_ref[POOL_BUF + t]
        acc16 = cur
        sums = {}
        for i in range(1, max(POOL_WINDOWS)):
            acc16 = acc16 + ext_ref[POOL_BUF + t - i]
            if i + 1 in POOL_WINDOWS:
                sums[i + 1] = acc16
        d = sums[POOL_WINDOWS[-1]] * (1.0 / POOL_WINDOWS[-1])
        for j, win in enumerate(POOL_WINDOWS[:-1]):
            d = jnp.where(gi == j, sums[win] * (1.0 / win), d)
        ds.append((d - cur).astype(BF16))
    h_ref[...] = _mm(jnp.concatenate(ds, axis=0), w_ref[0]) * sc_ref[...]


def _pool_sample(ext, w, scale):
    n, nb, d = ext.shape
    ng = len(POOL_WINDOWS)
    gw = d // ng
    return pl.pallas_call(
        _pool_sample_kernel,
        grid=(ng,),
        in_specs=[pl.BlockSpec((n, nb, gw), lambda gi: (0, 0, gi)),
                  pl.BlockSpec((1, gw, gw), lambda gi: (gi, 0, 0)),
                  pl.BlockSpec((1, gw), lambda gi: (0, gi))],
        out_specs=pl.BlockSpec(((n - POOL_BUF) * nb, gw), lambda gi: (0, gi)),
        out_shape=jax.ShapeDtypeStruct(((n - POOL_BUF) * nb, d), F32),
        compiler_params=_cparams("parallel"),
        name="pool_sample",
    )(ext, w, scale)


def _add_ln_kernel(alpha, o_in_ref, ob_in_ref, x_ref, h_ref, g_ref, b_ref, o_ref, ob_ref):
    del o_in_ref, ob_in_ref
    y = _layer_norm(alpha * x_ref[...] + h_ref[...], g_ref[...], b_ref[...])
    o_ref[...] = y
    ob_ref[...] = y.astype(BF16)


def _add_ln_rows(alpha, out, out_b, x, h, g, b, row0):
    m, d = x.shape
    rows = h.shape[0]
    blk = row0 // rows
    tile = pl.BlockSpec((rows, d), lambda i: (blk, 0))
    row = pl.BlockSpec((1, d), lambda i: (0, 0))
    anyspec = pl.BlockSpec(memory_space=pl.ANY)
    return pl.pallas_call(
        functools.partial(_add_ln_kernel, alpha),
        grid=(1,),
        in_specs=[anyspec, anyspec, tile, pl.BlockSpec((rows, d), lambda i: (0, 0)), row, row],
        out_specs=[tile, tile],
        out_shape=[jax.ShapeDtypeStruct((m, d), F32), jax.ShapeDtypeStruct((m, d), BF16)],
        input_output_aliases={0: 0, 1: 1},
        compiler_params=_cparams("arbitrary"),
        name="add_ln_rows",
    )(out, out_b, x, h, g, b)


def _pad_lora(w_a, w_b):
    rank = w_a.shape[1]
    pad = (-rank) % LORA_PAD
    return (jnp.pad(w_a, ((0, 0), (0, pad))).astype(BF16), jnp.pad(w_b, ((0, pad), (0, 0))).astype(BF16))


def kernel(x_prompt, x_sample, state_wkv, state_shift, state_pool, ln_g, ln_b, rw_mu, rw_wr, rw_wk, rw_wv, rw_wo, rw_w0, rw_w1, rw_w2, rw_a0, rw_a1, rw_a2, rw_v0, rw_v1, rw_v2, rw_g1, rw_g2, rw_kk, rw_ka, rw_rk, rw_gn_g, rw_gn_b, pool_w, pool_scale, ffn_w_in, ffn_w_down):
    bp, seq, d = x_prompt.shape
    bs, steps, _ = x_sample.shape
    depth = ln_g.shape[0]
    n_heads = d // HEAD
    n_mixers = 2
    alpha = float((2 * depth) ** 0.25)
    m_prompt = bp * seq
    m_sample = bs * steps
    nblk = bs // SEQ_PER_CHUNK
    assert steps * SEQ_PER_CHUNK == CHUNK and bs % SEQ_PER_CHUNK == 0
    assert seq % WKV_ROWS == 0 and seq % TM == 0 and m_sample == TM

    def sample_to_rows(a):
        return a.reshape(nblk, SEQ_PER_CHUNK, steps, d).transpose(0, 2, 1, 3).reshape(m_sample, d)

    def rows_to_sample(a):
        return a.reshape(nblk, steps, SEQ_PER_CHUNK, d).transpose(0, 2, 1, 3).reshape(bs, steps, d)

    x = jnp.concatenate([x_prompt.reshape(m_prompt, d), sample_to_rows(x_sample)], axis=0)
    xb = x.astype(BF16)

    new_wkv_p, new_wkv_s, new_shift_p, new_shift_s, new_pool_p, new_pool_s = [], [], [], [], [], []
    v_first = None
    for i in range(depth):
        j = i // n_mixers
        xp3 = x[:m_prompt].reshape(bp, seq, d)
        xs4 = x[m_prompt:].reshape(nblk, steps, SEQ_PER_CHUNK, d)
        if i % n_mixers == 0:
            prev_p = jnp.concatenate([jnp.zeros((bp, 1, d), F32), xp3[:, :-1]], axis=1).reshape(m_prompt, d)
            prev_s = jnp.concatenate([state_shift[j].reshape(nblk, 1, SEQ_PER_CHUNK, d), xs4[:, :-1]],
                                     axis=1).reshape(m_sample, d)
            xprev = jnp.concatenate([prev_p, prev_s], axis=0)
            new_shift_p.append(xp3[:, -1])
            new_shift_s.append(xs4[:, -1].reshape(bs, d))

            w1, w2 = _pad_lora(rw_w1[j], rw_w2[j])
            a1, a2 = _pad_lora(rw_a1[j], rw_a2[j])
            g1, g2 = rw_g1[j].astype(BF16), rw_g2[j].astype(BF16)
            if j == 0:
                v0 = v1 = v2 = None
            else:
                v0 = rw_v0[j - 1][None, :]
                v1, v2 = _pad_lora(rw_v1[j - 1], rw_v2[j - 1])
            mixed = _rwkv_mix(x, xprev, rw_mu[j], rw_w0[j][None, :], rw_a0[j][None, :], v0,
                              w1, w2, a1, a2, v1, v2, g1, g2)
            xr, xk, xv, lw, ag, g = mixed[:6]
            r, k, v = _rkv_proj(xr, xk, xv, rw_wr[j].astype(BF16), rw_wk[j].astype(BF16), rw_wv[j].astype(BF16))
            vres = None if j == 0 else (v_first, mixed[6])
            if j == 0:
                v_first = v
            tok = (r, k, v, lw, ag, g)
            chan = (rw_kk[j][None, :], rw_ka[j][None, :], rw_rk[j].reshape(1, d),
                    rw_gn_g[j][None, :], rw_gn_b[j][None, :])
            z, s_blk = _wkv_prompt(tok, vres, chan, bp, seq, m_prompt + m_sample)
            z, s_new = _wkv_sample(z, tok, vres, chan, state_wkv[j], m_prompt)
            s5 = s_blk.reshape(bp, d // PAIR, 2, HEAD, 2, HEAD)
            new_wkv_p.append(jnp.stack([s5[:, :, 0, :, 0, :], s5[:, :, 1, :, 1, :]], axis=2)
                             .reshape(bp, n_heads, HEAD, HEAD))
            new_wkv_s.append(s_new)
            x, xb = _out_ln(alpha, z, x, rw_wo[j].astype(BF16), ln_g[i, 0][None, :], ln_b[i, 0][None, :])
        else:
            xs_tb = xs4.transpose(1, 0, 2, 3).reshape(steps, bs, d)
            new_pool_p.append(xp3[:, -POOL_BUF:])
            new_pool_s.append(jnp.concatenate([state_pool[j][:, steps:], xs_tb.transpose(1, 0, 2)], axis=1))
            pw = pool_w[j].astype(BF16)
            sc = pool_scale[j][None, :]
            lg, lb = ln_g[i, 0][None, :], ln_b[i, 0][None, :]
            ext = jnp.concatenate([state_pool[j].transpose(1, 0, 2), xs_tb], axis=0)
            h_tb = _pool_sample(ext, pw, sc)
            h_rows = h_tb.reshape(steps, nblk, SEQ_PER_CHUNK, d).transpose(1, 0, 2, 3).reshape(m_sample, d)
            y, yb = _pool_prompt(alpha, x, pw, sc, lg, lb, m_prompt, seq)
            x, xb = _add_ln_rows(alpha, y, yb, x, h_rows, lg, lb, m_prompt)
        x, xb = _ffn(alpha, xb, x, ffn_w_in[i].astype(BF16), ffn_w_down[i].astype(BF16),
                     ln_g[i, 1][None, :], ln_b[i, 1][None, :])

    y_prompt = x[:m_prompt].reshape(bp, seq, d)
    y_sample = rows_to_sample(x[m_prompt:])
    return (y_prompt, y_sample, jnp.stack(new_wkv_p), jnp.stack(new_shift_p), jnp.stack(new_pool_p),
            jnp.stack(new_wkv_s), jnp.stack(new_shift_s), jnp.stack(new_pool_s))
```

```python
import functools
import math

import jax
import jax.numpy as jnp
from jax import lax
from jax.experimental import pallas as pl
from jax.experimental.pallas import tpu as pltpu

F32 = jnp.float32
BF16 = jnp.bfloat16

HEAD = 64
PAIR = 2 * HEAD
CHUNK = 64
SEQ_PER_CHUNK = 16
GN_EPS = 64e-5
LN_EPS = 1e-5
POOL_WINDOWS = (2, 4, 8, 16)
POOL_BUF = 15
HALO = 16
SHIFT_HALO = 8
LORA_PAD = 128

TM = 512
TM_MIX = 256
TN = 512
TF = 512
WKV_ROWS = 128
WKV_PAIRS = 16
WKV_SAMPLE_PAIRS = 4
VMEM_LIMIT = 56 * 1024 * 1024


def _cparams(*sem):
    return pltpu.CompilerParams(dimension_semantics=sem, vmem_limit_bytes=VMEM_LIMIT)


def _mm(a, b):
    return jnp.dot(a, b, preferred_element_type=F32)


def _mm_nt(a, b):
    return lax.dot_general(a, b, (((1,), (1,)), ((), ())), preferred_element_type=F32)


def _mm_tn(a, b):
    return lax.dot_general(a, b, (((0,), (0,)), ((), ())), preferred_element_type=F32)


def _sigmoid(x):
    return 1.0 / (1.0 + jnp.exp(-x))


def _layer_norm(v, g, b):
    mu = jnp.mean(v, axis=-1, keepdims=True)
    c = v - mu
    var = jnp.mean(c * c, axis=-1, keepdims=True)
    return c * lax.rsqrt(var + LN_EPS) * g + b


def _split3(x):
    h1 = x.astype(BF16)
    r1 = x - h1.astype(F32)
    h2 = r1.astype(BF16)
    h3 = (r1 - h2.astype(F32)).astype(BF16)
    return h1, h2, h3


def _mix_kernel(has_vres, prompt_tiles, tiles_per_seq, *refs):
    if has_vres:
        (x_ref, halo_ref, xps_ref, mu_ref, w0_ref, a0_ref, v0_ref, w1_ref, w2_ref, a1_ref, a2_ref,
         v1_ref, v2_ref, g1_ref, g2_ref, xr_ref, xk_ref, xv_ref, lw_ref, ag_ref, g_ref, vg_ref) = refs
    else:
        (x_ref, halo_ref, xps_ref, mu_ref, w0_ref, a0_ref, w1_ref, w2_ref, a1_ref, a2_ref,
         g1_ref, g2_ref, xr_ref, xk_ref, xv_ref, lw_ref, ag_ref, g_ref) = refs
    i = pl.program_id(0)
    x = x_ref[...]
    before = jnp.where(i % tiles_per_seq == 0, 0.0, halo_ref[SHIFT_HALO - 1:SHIFT_HALO, :])
    first_row = lax.broadcasted_iota(jnp.int32, x.shape, 0) == 0
    shifted = jnp.where(first_row, before, pltpu.roll(x, 1, 0))
    xx = jnp.where(i >= prompt_tiles, xps_ref[...], shifted) - x

    def mix(j):
        return (x + xx * mu_ref[j:j + 1, :]).astype(BF16)

    xr_ref[...] = mix(0)
    xk_ref[...] = mix(2)
    xv = mix(3)
    xv_ref[...] = xv

    u = w0_ref[...] + _mm(jnp.tanh(_mm(mix(1), w1_ref[...])).astype(BF16), w2_ref[...])
    z = -u
    softplus = jnp.maximum(z, 0.0) + jnp.log(1.0 + jnp.exp(-jnp.abs(z)))
    lw_ref[...] = -jnp.exp(-softplus - 0.5)
    ag_ref[...] = _sigmoid(a0_ref[...] + _mm(_mm(mix(4), a1_ref[...]).astype(BF16), a2_ref[...]))
    g_ref[...] = _mm(_sigmoid(_mm(mix(5), g1_ref[...])).astype(BF16), g2_ref[...])
    if has_vres:
        vg_ref[...] = _sigmoid(v0_ref[...] + _mm(_mm(xv, v1_ref[...]).astype(BF16), v2_ref[...]))


def _rwkv_mix(x, xprev_sample, m_prompt, seq, mu, w0, a0, v0, w1, w2, a1, a2, v1, v2, g1, g2):
    m, d = x.shape
    has_vres = v0 is not None
    prompt_tiles = m_prompt // TM_MIX
    tile = pl.BlockSpec((TM_MIX, d), lambda i: (i, 0))
    halo = pl.BlockSpec((SHIFT_HALO, d), lambda i: (jnp.maximum(i * (TM_MIX // SHIFT_HALO) - 1, 0), 0))
    xps = pl.BlockSpec((TM_MIX, d), lambda i: (jnp.maximum(i - prompt_tiles, 0), 0))

    def full(a):
        return pl.BlockSpec(a.shape, lambda i: (0,) * a.ndim)

    if has_vres:
        params = (mu, w0, a0, v0, w1, w2, a1, a2, v1, v2, g1, g2)
    else:
        params = (mu, w0, a0, w1, w2, a1, a2, g1, g2)
    n_f32_out = 4 if has_vres else 3
    out_shape = [jax.ShapeDtypeStruct((m, d), BF16)] * 3 + [jax.ShapeDtypeStruct((m, d), F32)] * n_f32_out
    return pl.pallas_call(
        functools.partial(_mix_kernel, has_vres, prompt_tiles, seq // TM_MIX),
        grid=(m // TM_MIX,),
        in_specs=[tile, halo, xps] + [full(a) for a in params],
        out_specs=[tile] * len(out_shape),
        out_shape=out_shape,
        compiler_params=_cparams("parallel"),
        name="rwkv_mix",
    )(x, x, xprev_sample, *params)


def _rkv_kernel(xr_ref, xk_ref, xv_ref, wr_ref, wk_ref, wv_ref, r_ref, k_ref, v_ref):
    r_ref[...] = _mm(xr_ref[...], wr_ref[...])
    k_ref[...] = _mm(xk_ref[...], wk_ref[...])
    v_ref[...] = _mm(xv_ref[...], wv_ref[...])


def _rkv_proj(xr, xk, xv, wr, wk, wv):
    m, d = xr.shape
    xs = pl.BlockSpec((TM, d), lambda n, i: (i, 0))
    ws = pl.BlockSpec((d, TN), lambda n, i: (0, n))
    os = pl.BlockSpec((TM, TN), lambda n, i: (i, n))
    return pl.pallas_call(
        _rkv_kernel,
        grid=(d // TN, m // TM),
        in_specs=[xs, xs, xs, ws, ws, ws],
        out_specs=[os, os, os],
        out_shape=[jax.ShapeDtypeStruct((m, d), F32)] * 3,
        compiler_params=_cparams("parallel", "arbitrary"),
        name="rkv_proj",
    )(xr, xk, xv, wr, wk, wv)


class _ChunkMasks:
    def __init__(self, sample):
        row = lax.broadcasted_iota(jnp.int32, (CHUNK, PAIR), 0)
        lane = lax.broadcasted_iota(jnp.int32, (CHUNK, PAIR), 1)
        col = lane & (CHUNK - 1)
        r2 = lax.broadcasted_iota(jnp.int32, (CHUNK, CHUNK), 0)
        c2 = lax.broadcasted_iota(jnp.int32, (CHUNK, CHUNK), 1)
        if sample:
            shift = int(math.log2(SEQ_PER_CHUNK))
            seq_mask = SEQ_PER_CHUNK - 1
            same = (row & seq_mask) == (col & seq_mask)
            self.strict = same & ((col >> shift) < (row >> shift))
            self.incl = same & ((col >> shift) <= (row >> shift))
            tri = ((r2 & seq_mask) == (c2 & seq_mask)) & ((c2 >> shift) <= (r2 >> shift))
            self.doublings = 1
        else:
            self.strict = col < row
            self.incl = col <= row
            tri = c2 <= r2
            self.doublings = 5
        self.sample = sample
        self.eye = jnp.where(row == col, 1.0, 0.0).astype(F32)
        self.tri = jnp.where(tri, 1.0, 0.0).astype(BF16)
        self.head0 = lane < HEAD
        r128 = lax.broadcasted_iota(jnp.int32, (PAIR, PAIR), 0)
        c128 = lax.broadcasted_iota(jnp.int32, (PAIR, PAIR), 1)
        self.same_head = (r128 < HEAD) == (c128 < HEAD)
        self.diag128 = r128 == c128


def _blk(y, cm):
    return jnp.concatenate([jnp.where(cm.head0, y, 0.0), jnp.where(cm.head0, 0.0, y)], axis=0).astype(BF16)


def _headsum(xs, cm):
    parts = [(jnp.sum(jnp.where(cm.head0, x, 0.0), axis=1, keepdims=True),
              jnp.sum(jnp.where(cm.head0, 0.0, x), axis=1, keepdims=True)) for x in xs]
    return [jnp.where(cm.head0, s0, s1) for s0, s1 in parts]


def _wkv_pre(kraw, v, ag, vf, vg, kkw, ka, cm):
    kkv = [k * w for k, w in zip(kraw, kkw)]
    sumsq = _headsum([x * x for x in kkv], cm)
    kkn = [x / jnp.maximum(jnp.sqrt(s), 1e-12) for x, s in zip(kkv, sumsq)]
    kmod = [k * (1.0 + (g - 1.0) * c) for k, g, c in zip(kraw, ag, ka)]
    if vf is not None:
        v = [x + (f - x) * g for x, f, g in zip(v, vf, vg)]
    return kmod, v, [-x for x in kkn], [x * g for x, g in zip(kkn, ag)]


def _wkv_post(y, r, kmod, v, g, rk, gng, gnb, cm):
    inv_n = 1.0 / HEAD
    n = len(y)
    sums = _headsum(list(y) + [a * b * c for a, b, c in zip(r, kmod, rk)], cm)
    yc = [a - s * inv_n for a, s in zip(y, sums[:n])]
    var = _headsum([c * c for c in yc], cm)
    return [((c * lax.rsqrt(s * inv_n + GN_EPS) * gg + gb + bs * vv) * gt).astype(BF16)
            for c, s, gg, gb, bs, vv, gt in zip(yc, var, gng, gnb, sums[n:], v, g)]


def _chunk_prep(r, lw, k, v, a, b, cm):
    splits = [_split3(x) for x in lw]
    cum = [_mm(cm.tri, s[0]) + _mm(cm.tri, s[1]) + _mm(cm.tri, s[2]) for s in splits]
    if cm.sample:
        cl = [jnp.concatenate([c[CHUNK - SEQ_PER_CHUNK:, :]] * (CHUNK // SEQ_PER_CHUNK), axis=0) for c in cum]
    else:
        cl = [jnp.broadcast_to(c[CHUNK - 1:CHUNK, :], c.shape) for c in cum]
    w_inv = [jnp.exp(-c) for c in cum]
    w_last = [jnp.exp(c) for c in cl]
    rt = [x * jnp.exp(c) for x, c in zip(r, cum)]
    at = [x * jnp.exp(c - l) for x, c, l in zip(a, cum, lw)]
    bt = [x * w for x, w in zip(b, w_inv)]
    kt = [x * w for x, w in zip(k, w_inv)]
    x = [jnp.concatenate([p, q], axis=0).astype(BF16) for p, q in zip(at, rt)]
    blk_b = [_blk(y, cm) for y in bt]
    blk_k = [_blk(y, cm) for y in kt]
    ab_rb = [_mm_nt(p, q) for p, q in zip(x, blk_b)]
    ak_rk = [_mm_nt(p, q) for p, q in zip(x, blk_k)]
    nak = [jnp.where(cm.strict, y[:CHUNK], 0.0).astype(BF16) for y in ak_rk]
    mrb = [jnp.where(cm.incl, y[CHUNK:], 0.0).astype(BF16) for y in ab_rb]
    mrk = [jnp.where(cm.incl, y[CHUNK:], 0.0).astype(BF16) for y in ak_rk]
    p = [jnp.where(cm.strict, y[:CHUNK], 0.0) for y in ab_rb]
    t = [cm.eye + y for y in p]
    for _ in range(cm.doublings):
        p = [_mm(y.astype(BF16), _blk(y, cm)) for y in p]
        t = [y + _mm(y.astype(BF16), _blk(q, cm)) for y, q in zip(t, p)]
    tb = [y.astype(BF16) for y in t]
    blk_v = [_blk(y, cm) for y in v]
    q = [_mm(m, w) for m, w in zip(nak, blk_v)]
    ah = [_mm(m, _blk(y, cm)) for m, y in zip(tb, at)]
    uh = [_mm(m, _blk(y, cm)) for m, y in zip(tb, q)]
    rh = [y + _mm(m, _blk(z, cm)) for y, m, z in zip(rt, mrb, ah)]
    yh = [_mm(m, _blk(z, cm)) + _mm(n, w) for m, z, n, w in zip(mrb, uh, mrk, blk_v)]
    bh = [y * w for y, w in zip(bt, w_last)]
    kh = [y * w for y, w in zip(kt, w_last)]
    return ah, uh, rh, yh, bh, kh, w_last


def _wkv_prompt_kernel(has_vres, n_chunks, n_pairs, *refs):
    if has_vres:
        (r_ref, k_ref, v_ref, lw_ref, ag_ref, g_ref, vf_ref, vg_ref,
         kkw_ref, ka_ref, rk_ref, gng_ref, gnb_ref, z_ref, s_ref, s_scr) = refs
    else:
        (r_ref, k_ref, v_ref, lw_ref, ag_ref, g_ref,
         kkw_ref, ka_ref, rk_ref, gng_ref, gnb_ref, z_ref, s_ref, s_scr) = refs
    step = pl.program_id(1)

    @pl.when(step == 0)
    def _():
        s_scr[...] = jnp.zeros_like(s_scr)

    cm = _ChunkMasks(sample=False)
    lanes = [slice(p * PAIR, (p + 1) * PAIR) for p in range(n_pairs)]

    def chunk(c, carry):
        rows = pl.ds(pl.multiple_of(c * CHUNK, CHUNK), CHUNK)

        def tok(ref):
            return [ref[rows, l] for l in lanes]

        def chan(ref):
            return [ref[:, l] for l in lanes]

        r = tok(r_ref)
        kmod, v, a, b = _wkv_pre(tok(k_ref), tok(v_ref), tok(ag_ref),
                                 tok(vf_ref) if has_vres else None, tok(vg_ref) if has_vres else None,
                                 chan(kkw_ref), chan(ka_ref), cm)
        ah, uh, rh, yh, bh, kh, w_last = _chunk_prep(r, tok(lw_ref), kmod, v, a, b, cm)
        s = [s_scr[p] for p in range(n_pairs)]
        s_hi = [x.astype(BF16) for x in s]
        s_lo = [(x - h.astype(F32)).astype(BF16) for x, h in zip(s, s_hi)]
        y = [_mm_nt(x.astype(BF16), h) + z for x, h, z in zip(rh, s_hi, yh)]
        bh16 = [x.astype(BF16) for x in bh]
        ab = [_mm_tn(x.astype(BF16), w) for x, w in zip(ah, bh16)]
        uvbk = [_mm_tn(jnp.concatenate([p, q], axis=0).astype(BF16),
                       jnp.concatenate([w, x.astype(BF16)], axis=0))
                for p, q, w, x in zip(uh, v, bh16, kh)]
        phi = [(jnp.where(cm.diag128, jnp.broadcast_to(w[:1, :], (PAIR, PAIR)), 0.0)
                + jnp.where(cm.same_head, m, 0.0)).astype(BF16) for w, m in zip(w_last, ab)]
        s_new = [_mm(h, f) + _mm(l, f) + jnp.where(cm.same_head, m, 0.0)
                 for h, l, f, m in zip(s_hi, s_lo, phi, uvbk)]
        for p in range(n_pairs):
            s_scr[p] = s_new[p]
        z = _wkv_post(y, r, kmod, v, tok(g_ref), chan(rk_ref), chan(gng_ref), chan(gnb_ref), cm)
        for l, zz in zip(lanes, z):
            z_ref[rows, l] = zz
        return carry

    lax.fori_loop(0, n_chunks, chunk, 0)

    @pl.when(step == pl.num_programs(1) - 1)
    def _():
        s_ref[0] = s_scr[...]


def _wkv_prompt(tok, vres, chan, batch, seq, m_total):
    d = tok[0].shape[1]
    has_vres = vres is not None
    assert d == WKV_PAIRS * PAIR
    steps = seq // WKV_ROWS
    tile = pl.BlockSpec((WKV_ROWS, d), lambda b, s: (b * steps + s, 0))
    row = pl.BlockSpec((1, d), lambda b, s: (0, 0))
    args = tuple(tok) + (tuple(vres) if has_vres else ()) + tuple(chan)
    n_tok = len(tok) + (2 if has_vres else 0)
    return pl.pallas_call(
        functools.partial(_wkv_prompt_kernel, has_vres, WKV_ROWS // CHUNK, WKV_PAIRS),
        grid=(batch, steps),
        in_specs=[tile] * n_tok + [row] * len(chan),
        out_specs=[tile, pl.BlockSpec((1, WKV_PAIRS, PAIR, PAIR), lambda b, s: (b, 0, 0, 0))],
        out_shape=[jax.ShapeDtypeStruct((m_total, d), BF16),
                   jax.ShapeDtypeStruct((batch, WKV_PAIRS, PAIR, PAIR), F32)],
        scratch_shapes=[pltpu.VMEM((WKV_PAIRS, PAIR, PAIR), F32)],
        compiler_params=_cparams("parallel", "arbitrary"),
        name="wkv_prompt",
    )(*args)


def _wkv_sample_kernel(has_vres, n_pairs, *refs):
    if has_vres:
        (z_in_ref, r_ref, k_ref, v_ref, lw_ref, ag_ref, g_ref, vf_ref, vg_ref,
         kkw_ref, ka_ref, rk_ref, gng_ref, gnb_ref, s_in_ref, z_ref, s_out_ref,
         x_scr, u_scr, bk_scr, gy_scr) = refs
    else:
        (z_in_ref, r_ref, k_ref, v_ref, lw_ref, ag_ref, g_ref,
         kkw_ref, ka_ref, rk_ref, gng_ref, gnb_ref, s_in_ref, z_ref, s_out_ref,
         x_scr, u_scr, bk_scr, gy_scr) = refs
    del z_in_ref
    cm = _ChunkMasks(sample=True)
    row8 = lax.broadcasted_iota(jnp.int32, (8, PAIR), 0)
    zeros_h = jnp.zeros((HEAD, HEAD), F32)
    steps = CHUNK // SEQ_PER_CHUNK
    lanes = [slice(p * PAIR, (p + 1) * PAIR) for p in range(n_pairs)]

    def tok(ref):
        return [ref[:, l] for l in lanes]

    r = tok(r_ref)
    kmod, v, a, b = _wkv_pre(tok(k_ref), tok(v_ref), tok(ag_ref),
                             tok(vf_ref) if has_vres else None, tok(vg_ref) if has_vres else None,
                             tok(kkw_ref), tok(ka_ref), cm)
    ah, uh, rh, yh, bh, kh, w_last = _chunk_prep(r, tok(lw_ref), kmod, v, a, b, cm)
    for p in range(n_pairs):
        x_scr[p, 0:CHUNK, :] = ah[p]
        x_scr[p, CHUNK:, :] = rh[p]
        u_scr[p, 0:CHUNK, :] = uh[p]
        u_scr[p, CHUNK:, :] = yh[p]
        bk_scr[p, 0:CHUNK, :] = bh[p]
        bk_scr[p, CHUNK:, :] = kh[p]
        gy_scr[p, CHUNK:, :] = v[p]

    seqs = range(SEQ_PER_CHUNK)
    for p in range(n_pairs):
        pick = [pl.ds(i, 2 * steps, stride=SEQ_PER_CHUNK) for i in seqs]
        s = [jnp.concatenate([jnp.concatenate([s_in_ref[i, 2 * p], zeros_h], axis=1),
                              jnp.concatenate([zeros_h, s_in_ref[i, 2 * p + 1]], axis=1)], axis=0) for i in seqs]
        uy = [_mm_nt(x_scr[p, pk, :].astype(BF16), m.astype(BF16)) + u_scr[p, pk, :] for pk, m in zip(pick, s)]
        uv = [jnp.where(row8 < steps, m, gy_scr[p, pk, :]) for pk, m in zip(pick, uy)]
        for pk, m in zip(pick, uy):
            gy_scr[p, pk, :] = m
        upd = [_mm_tn(m.astype(BF16), bk_scr[p, pk, :].astype(BF16)) for pk, m in zip(pick, uv)]
        for i in seqs:
            s_new = s[i] * w_last[p][i:i + 1, :] + jnp.where(cm.same_head, upd[i], 0.0)
            s_out_ref[i, 2 * p] = s_new[:HEAD, :HEAD]
            s_out_ref[i, 2 * p + 1] = s_new[HEAD:, HEAD:]

    y = [gy_scr[p, CHUNK:, :] for p in range(n_pairs)]
    z = _wkv_post(y, r, kmod, v, tok(g_ref), tok(rk_ref), tok(gng_ref), tok(gnb_ref), cm)
    for l, zz in zip(lanes, z):
        z_ref[:, l] = zz


def _wkv_sample(z, tok, vres, chan, state, row0):
    m, d = z.shape
    nb = state.shape[0]
    has_vres = vres is not None
    n_pairs = WKV_SAMPLE_PAIRS
    width = n_pairs * PAIR
    blk0 = row0 // CHUNK
    tile = pl.BlockSpec((CHUNK, width), lambda i, p: (blk0 + i, p))
    row = pl.BlockSpec((1, width), lambda i, p: (0, p))
    st = pl.BlockSpec((SEQ_PER_CHUNK, 2 * n_pairs, HEAD, HEAD), lambda i, p: (i, p, 0, 0))
    args = (z,) + tuple(tok) + (tuple(vres) if has_vres else ()) + tuple(chan) + (state,)
    n_tok = len(tok) + (2 if has_vres else 0)
    return pl.pallas_call(
        functools.partial(_wkv_sample_kernel, has_vres, n_pairs),
        grid=(nb // SEQ_PER_CHUNK, d // width),
        in_specs=[pl.BlockSpec(memory_space=pl.ANY)] + [tile] * n_tok + [row] * len(chan) + [st],
        out_specs=[tile, st],
        out_shape=[jax.ShapeDtypeStruct((m, d), BF16), jax.ShapeDtypeStruct(state.shape, F32)],
        scratch_shapes=[pltpu.VMEM((n_pairs, 2 * CHUNK, PAIR), F32)] * 4,
        input_output_aliases={0: 0},
        compiler_params=_cparams("parallel", "parallel"),
        name="wkv_sample",
    )(*args)


def _out_ln_kernel(alpha, z_ref, x_ref, w_ref, g_ref, b_ref, o_ref, ob_ref):
    y = _layer_norm(alpha * x_ref[...] + _mm(z_ref[...], w_ref[...]), g_ref[...], b_ref[...])
    o_ref[...] = y
    ob_ref[...] = y.astype(BF16)


def _out_ln(alpha, z, x, w, g, b):
    m, d = x.shape
    tile = pl.BlockSpec((TM, d), lambda i: (i, 0))
    row = pl.BlockSpec((1, d), lambda i: (0, 0))
    return pl.pallas_call(
        functools.partial(_out_ln_kernel, alpha),
        grid=(m // TM,),
        in_specs=[tile, tile, pl.BlockSpec((d, d), lambda i: (0, 0)), row, row],
        out_specs=[tile, tile],
        out_shape=[jax.ShapeDtypeStruct((m, d), F32), jax.ShapeDtypeStruct((m, d), BF16)],
        compiler_params=_cparams("parallel"),
        name="out_ln",
    )(z, x, w, g, b)


def _ffn_kernel(alpha, xb_ref, x_ref, wg_ref, wu_ref, wd_ref, g_ref, b_ref, o_ref, ob_ref, acc_ref):
    f = pl.program_id(1)
    xb = xb_ref[...]
    gate = _mm(xb, wg_ref[...])
    up = _mm(xb, wu_ref[...])
    part = _mm((gate * _sigmoid(gate) * up).astype(BF16), wd_ref[...])

    @pl.when(f == 0)
    def _():
        acc_ref[...] = part

    @pl.when(f > 0)
    def _():
        acc_ref[...] += part

    @pl.when(f == pl.num_programs(1) - 1)
    def _():
        y = _layer_norm(alpha * x_ref[...] + acc_ref[...], g_ref[...], b_ref[...])
        o_ref[...] = y
        ob_ref[...] = y.astype(BF16)


def _ffn(alpha, xb, x, w_in, w_down, g, b):
    m, d = x.shape
    d_ff = w_down.shape[0]
    nf = d_ff // TF
    tile = pl.BlockSpec((TM, d), lambda i, f: (i, 0))
    row = pl.BlockSpec((1, d), lambda i, f: (0, 0))
    return pl.pallas_call(
        functools.partial(_ffn_kernel, alpha),
        grid=(m // TM, nf),
        in_specs=[tile, tile,
                  pl.BlockSpec((d, TF), lambda i, f: (0, f)),
                  pl.BlockSpec((d, TF), lambda i, f: (0, f + nf)),
                  pl.BlockSpec((TF, d), lambda i, f: (f, 0)),
                  row, row],
        out_specs=[tile, tile],
        out_shape=[jax.ShapeDtypeStruct((m, d), F32), jax.ShapeDtypeStruct((m, d), BF16)],
        scratch_shapes=[pltpu.VMEM((TM, d), F32)],
        compiler_params=_cparams("parallel", "arbitrary"),
        name="ffn",
    )(xb, x, w_in, w_in, w_down, g, b)


def _pool_prompt_kernel(alpha, tiles_per_seq, x_ref, halo_ref, w_ref, sc_ref, g_ref, b_ref, o_ref, ob_ref):
    tile_in_seq = pl.program_id(0) % tiles_per_seq
    x = x_ref[...]
    halo = jnp.where(tile_in_seq == 0, 0.0, halo_ref[...])
    gw = x.shape[1] // len(POOL_WINDOWS)
    pos = tile_in_seq * TM + lax.broadcasted_iota(jnp.int32, (TM, gw), 0)
    outs = []
    for gi, win in enumerate(POOL_WINDOWS):
        lanes = slice(gi * gw, (gi + 1) * gw)
        xg = x[:, lanes]
        s = jnp.concatenate([halo[:, lanes], xg], axis=0)
        span = 1
        while span < win:
            s = s[span:] + s[:-span]
            span *= 2
        first = HALO - (win - 1)
        cnt = jnp.minimum(win, pos + 1).astype(F32)
        dg = s[first:first + TM] / cnt - xg
        outs.append(_mm(dg.astype(BF16), w_ref[gi]))
    h = jnp.concatenate(outs, axis=1) * sc_ref[...]
    y = _layer_norm(alpha * x + h, g_ref[...], b_ref[...])
    o_ref[...] = y
    ob_ref[...] = y.astype(BF16)


def _pool_prompt(alpha, x, w, scale, g, b, m_prompt, seq):
    m, d = x.shape
    tile = pl.BlockSpec((TM, d), lambda i: (i, 0))
    row = pl.BlockSpec((1, d), lambda i: (0, 0))
    halo = pl.BlockSpec((HALO, d), lambda i: (jnp.maximum(i * (TM // HALO) - 1, 0), 0))
    return pl.pallas_call(
        functools.partial(_pool_prompt_kernel, alpha, seq // TM),
        grid=(m_prompt // TM,),
        in_specs=[tile, halo, pl.BlockSpec(w.shape, lambda i: (0, 0, 0)), row, row, row],
        out_specs=[tile, tile],
        out_shape=[jax.ShapeDtypeStruct((m, d), F32), jax.ShapeDtypeStruct((m, d), BF16)],
        compiler_params=_cparams("parallel"),
        name="pool_prompt",
    )(x, x, w, scale, g, b)


def _pool_sample_kernel(ext_ref, w_ref, sc_ref, h_ref):
    steps = ext_ref.shape[0] - POOL_BUF
    gi = pl.program_id(0)
    ds = []
    for t in range(steps):
        cur = ext_ref[POOL_BUF + t]
        acc16 = cur
        sums = {}
        for i in range(1, max(POOL_WINDOWS)):
            acc16 = acc16 + ext_ref[POOL_BUF + t - i]
            if i + 1 in POOL_WINDOWS:
                sums[i + 1] = acc16
        d = sums[POOL_WINDOWS[-1]] * (1.0 / POOL_WINDOWS[-1])
        for j, win in enumerate(POOL_WINDOWS[:-1]):
            d = jnp.where(gi == j, sums[win] * (1.0 / win), d)
        ds.append((d - cur).astype(BF16))
    h_ref[...] = _mm(jnp.concatenate(ds, axis=0), w_ref[0]) * sc_ref[...]


def _pool_sample(ext, w, scale):
    n, nb, d = ext.shape
    ng = len(POOL_WINDOWS)
    gw = d // ng
    return pl.pallas_call(
        _pool_sample_kernel,
        grid=(ng,),
        in_specs=[pl.BlockSpec((n, nb, gw), lambda gi: (0, 0, gi)),
                  pl.BlockSpec((1, gw, gw), lambda gi: (gi, 0, 0)),
                  pl.BlockSpec((1, gw), lambda gi: (0, gi))],
        out_specs=pl.BlockSpec(((n - POOL_BUF) * nb, gw), lambda gi: (0, gi)),
        out_shape=jax.ShapeDtypeStruct(((n - POOL_BUF) * nb, d), F32),
        compiler_params=_cparams("parallel"),
        name="pool_sample",
    )(ext, w, scale)


def _add_ln_kernel(alpha, o_in_ref, ob_in_ref, x_ref, h_ref, g_ref, b_ref, o_ref, ob_ref):
    del o_in_ref, ob_in_ref
    y = _layer_norm(alpha * x_ref[...] + h_ref[...], g_ref[...], b_ref[...])
    o_ref[...] = y
    ob_ref[...] = y.astype(BF16)


def _add_ln_rows(alpha, out, out_b, x, h, g, b, row0):
    m, d = x.shape
    rows = h.shape[0]
    blk = row0 // rows
    tile = pl.BlockSpec((rows, d), lambda i: (blk, 0))
    row = pl.BlockSpec((1, d), lambda i: (0, 0))
    anyspec = pl.BlockSpec(memory_space=pl.ANY)
    return pl.pallas_call(
        functools.partial(_add_ln_kernel, alpha),
        grid=(1,),
        in_specs=[anyspec, anyspec, tile, pl.BlockSpec((rows, d), lambda i: (0, 0)), row, row],
        out_specs=[tile, tile],
        out_shape=[jax.ShapeDtypeStruct((m, d), F32), jax.ShapeDtypeStruct((m, d), BF16)],
        input_output_aliases={0: 0, 1: 1},
        compiler_params=_cparams("arbitrary"),
        name="add_ln_rows",
    )(out, out_b, x, h, g, b)


def _pad_lora(w_a, w_b):
    rank = w_a.shape[1]
    pad = (-rank) % LORA_PAD
    return (jnp.pad(w_a, ((0, 0), (0, pad))).astype(BF16), jnp.pad(w_b, ((0, pad), (0, 0))).astype(BF16))


def kernel(x_prompt, x_sample, state_wkv, state_shift, state_pool, ln_g, ln_b, rw_mu, rw_wr, rw_wk, rw_wv, rw_wo, rw_w0, rw_w1, rw_w2, rw_a0, rw_a1, rw_a2, rw_v0, rw_v1, rw_v2, rw_g1, rw_g2, rw_kk, rw_ka, rw_rk, rw_gn_g, rw_gn_b, pool_w, pool_scale, ffn_w_in, ffn_w_down):
    bp, seq, d = x_prompt.shape
    bs, steps, _ = x_sample.shape
    depth = ln_g.shape[0]
    n_heads = d // HEAD
    n_mixers = 2
    alpha = float((2 * depth) ** 0.25)
    m_prompt = bp * seq
    m_sample = bs * steps
    nblk = bs // SEQ_PER_CHUNK
    assert steps * SEQ_PER_CHUNK == CHUNK and bs % SEQ_PER_CHUNK == 0
    assert seq % WKV_ROWS == 0 and seq % TM == 0 and m_sample == TM

    def sample_to_rows(a):
        return a.reshape(nblk, SEQ_PER_CHUNK, steps, d).transpose(0, 2, 1, 3).reshape(m_sample, d)

    def rows_to_sample(a):
        return a.reshape(nblk, steps, SEQ_PER_CHUNK, d).transpose(0, 2, 1, 3).reshape(bs, steps, d)

    x = jnp.concatenate([x_prompt.reshape(m_prompt, d), sample_to_rows(x_sample)], axis=0)
    xb = None

    new_wkv_p, new_wkv_s, new_shift_p, new_shift_s, new_pool_p, new_pool_s = [], [], [], [], [], []
    v_first = None
    for i in range(depth):
        j = i // n_mixers
        xs4 = x[m_prompt:].reshape(nblk, steps, SEQ_PER_CHUNK, d)
        if i % n_mixers == 0:
            prev_s = jnp.concatenate([state_shift[j].reshape(nblk, 1, SEQ_PER_CHUNK, d), xs4[:, :-1]],
                                     axis=1).reshape(m_sample, d)
            new_shift_p.append(x[seq - 1:m_prompt:seq])
            new_shift_s.append(xs4[:, -1].reshape(bs, d))

            w1, w2 = _pad_lora(rw_w1[j], rw_w2[j])
            a1, a2 = _pad_lora(rw_a1[j], rw_a2[j])
            g1, g2 = rw_g1[j].astype(BF16), rw_g2[j].astype(BF16)
            if j == 0:
                v0 = v1 = v2 = None
            else:
                v0 = rw_v0[j - 1][None, :]
                v1, v2 = _pad_lora(rw_v1[j - 1], rw_v2[j - 1])
            mixed = _rwkv_mix(x, prev_s, m_prompt, seq, rw_mu[j], rw_w0[j][None, :], rw_a0[j][None, :], v0,
                              w1, w2, a1, a2, v1, v2, g1, g2)
            xr, xk, xv, lw, ag, g = mixed[:6]
            r, k, v = _rkv_proj(xr, xk, xv, rw_wr[j].astype(BF16), rw_wk[j].astype(BF16), rw_wv[j].astype(BF16))
            vres = None if j == 0 else (v_first, mixed[6])
            if j == 0:
                v_first = v
            tok = (r, k, v, lw, ag, g)
            chan = (rw_kk[j][None, :], rw_ka[j][None, :], rw_rk[j].reshape(1, d),
                    rw_gn_g[j][None, :], rw_gn_b[j][None, :])
            z, s_blk = _wkv_prompt(tok, vres, chan, bp, seq, m_prompt + m_sample)
            z, s_new = _wkv_sample(z, tok, vres, chan, state_wkv[j], m_prompt)
            s5 = s_blk.reshape(bp, d // PAIR, 2, HEAD, 2, HEAD)
            new_wkv_p.append(jnp.stack([s5[:, :, 0, :, 0, :], s5[:, :, 1, :, 1, :]], axis=2)
                             .reshape(bp, n_heads, HEAD, HEAD))
            new_wkv_s.append(s_new)
            x, xb = _out_ln(alpha, z, x, rw_wo[j].astype(BF16), ln_g[i, 0][None, :], ln_b[i, 0][None, :])
        else:
            xs_tb = xs4.transpose(1, 0, 2, 3).reshape(steps, bs, d)
            new_pool_p.append(jnp.stack([x[(b + 1) * seq - POOL_BUF:(b + 1) * seq] for b in range(bp)]))
            new_pool_s.append(jnp.concatenate([state_pool[j][:, steps:], xs_tb.transpose(1, 0, 2)], axis=1))
            pw = pool_w[j].astype(BF16)
            sc = pool_scale[j][None, :]
            lg, lb = ln_g[i, 0][None, :], ln_b[i, 0][None, :]
            ext = jnp.concatenate([state_pool[j].transpose(1, 0, 2), xs_tb], axis=0)
            h_tb = _pool_sample(ext, pw, sc)
            h_rows = h_tb.reshape(steps, nblk, SEQ_PER_CHUNK, d).transpose(1, 0, 2, 3).reshape(m_sample, d)
            y, yb = _pool_prompt(alpha, x, pw, sc, lg, lb, m_prompt, seq)
            x, xb = _add_ln_rows(alpha, y, yb, x, h_rows, lg, lb, m_prompt)
        x, xb = _ffn(alpha, xb, x, ffn_w_in[i].astype(BF16), ffn_w_down[i].astype(BF16),
                     ln_g[i, 1][None, :], ln_b[i, 1][None, :])

    y_prompt = x[:m_prompt].reshape(bp, seq, d)
    y_sample = rows_to_sample(x[m_prompt:])
    return (y_prompt, y_sample, jnp.stack(new_wkv_p), jnp.stack(new_shift_p), jnp.stack(new_pool_p),
            jnp.stack(new_wkv_s), jnp.stack(new_shift_s), jnp.stack(new_pool_s))
```

```python
import functools
import math

import jax
import jax.numpy as jnp
from jax import lax
from jax.experimental import pallas as pl
from jax.experimental.pallas import tpu as pltpu

F32 = jnp.float32
BF16 = jnp.bfloat16

HEAD = 64
PAIR = 2 * HEAD
CHUNK = 64
SEQ_PER_CHUNK = 16
GN_EPS = 64e-5
LN_EPS = 1e-5
POOL_WINDOWS = (2, 4, 8, 16)
POOL_BUF = 15
HALO = 16
SHIFT_HALO = 8
LORA_PAD = 128

TM = 512
TM_MIX = 256
TN = 512
TF = 512
FFN_DOWN_CHUNK = 512
WKV_ROWS = 128
WKV_PAIRS = 16
WKV_SAMPLE_PAIRS = 4
VMEM_LIMIT = 56 * 1024 * 1024


def _cparams(*sem):
    return pltpu.CompilerParams(dimension_semantics=sem, vmem_limit_bytes=VMEM_LIMIT)


def _mm(a, b):
    return jnp.dot(a, b, preferred_element_type=F32)


def _mm_nt(a, b):
    return lax.dot_general(a, b, (((1,), (1,)), ((), ())), preferred_element_type=F32)


def _mm_tn(a, b):
    return lax.dot_general(a, b, (((0,), (0,)), ((), ())), preferred_element_type=F32)


def _sigmoid(x):
    return 1.0 / (1.0 + jnp.exp(-x))


def _layer_norm(v, g, b):
    mu = jnp.mean(v, axis=-1, keepdims=True)
    c = v - mu
    var = jnp.mean(c * c, axis=-1, keepdims=True)
    return c * lax.rsqrt(var + LN_EPS) * g + b


def _split2(x):
    hi = x.astype(BF16)
    return hi, (x - hi.astype(F32)).astype(BF16)


def _row(a):
    return pl.BlockSpec((1, a.shape[-1]), lambda *_: (0, 0))


def _mix_kernel(has_vres, tiles_per_seq, *refs):
    if has_vres:
        (x_ref, prev_ref, mu_ref, w0_ref, a0_ref, v0_ref, w1_ref, w2_ref, a1_ref, a2_ref,
         v1_ref, v2_ref, g1_ref, g2_ref, xr_ref, xk_ref, xv_ref, lw_ref, ag_ref, g_ref, vg_ref) = refs
    else:
        (x_ref, prev_ref, mu_ref, w0_ref, a0_ref, w1_ref, w2_ref, a1_ref, a2_ref,
         g1_ref, g2_ref, xr_ref, xk_ref, xv_ref, lw_ref, ag_ref, g_ref) = refs
    x = x_ref[...]
    if tiles_per_seq is None:
        xprev = prev_ref[...]
    else:
        before = jnp.where(pl.program_id(0) % tiles_per_seq == 0, 0.0, prev_ref[SHIFT_HALO - 1:SHIFT_HALO, :])
        first_row = lax.broadcasted_iota(jnp.int32, x.shape, 0) == 0
        xprev = jnp.where(first_row, before, pltpu.roll(x, 1, 0))
    xx = xprev - x

    def mix(j):
        return (x + xx * mu_ref[j:j + 1, :]).astype(BF16)

    xr_ref[...] = mix(0)
    xk_ref[...] = mix(2)
    xv = mix(3)
    xv_ref[...] = xv

    u = w0_ref[...] + _mm(jnp.tanh(_mm(mix(1), w1_ref[...])).astype(BF16), w2_ref[...])
    z = -u
    softplus = jnp.maximum(z, 0.0) + jnp.log(1.0 + jnp.exp(-jnp.abs(z)))
    lw_ref[...] = -jnp.exp(-softplus - 0.5)
    ag_ref[...] = _sigmoid(a0_ref[...] + _mm(_mm(mix(4), a1_ref[...]).astype(BF16), a2_ref[...]))
    g_ref[...] = _mm(_sigmoid(_mm(mix(5), g1_ref[...])).astype(BF16), g2_ref[...])
    if has_vres:
        vg_ref[...] = _sigmoid(v0_ref[...] + _mm(_mm(xv, v1_ref[...]).astype(BF16), v2_ref[...]))


def _rwkv_mix(x, xprev, seq, params):
    m, d = x.shape
    has_vres = len(params) == 12
    tile = pl.BlockSpec((TM_MIX, d), lambda i: (i, 0))
    if xprev is None:
        prev, prev_spec = x, pl.BlockSpec(
            (SHIFT_HALO, d), lambda i: (jnp.maximum(i * (TM_MIX // SHIFT_HALO) - 1, 0), 0))
        tiles_per_seq = seq // TM_MIX
    else:
        prev, prev_spec, tiles_per_seq = xprev, tile, None

    def full(a):
        return pl.BlockSpec(a.shape, lambda i: (0,) * a.ndim)

    n_f32_out = 4 if has_vres else 3
    out_shape = [jax.ShapeDtypeStruct((m, d), BF16)] * 3 + [jax.ShapeDtypeStruct((m, d), F32)] * n_f32_out
    return pl.pallas_call(
        functools.partial(_mix_kernel, has_vres, tiles_per_seq),
        grid=(m // TM_MIX,),
        in_specs=[tile, prev_spec] + [full(a) for a in params],
        out_specs=[tile] * len(out_shape),
        out_shape=out_shape,
        compiler_params=_cparams("parallel"),
        name="rwkv_mix",
    )(x, prev, *params)


def _rkv_kernel(xr_ref, xk_ref, xv_ref, wr_ref, wk_ref, wv_ref, r_ref, k_ref, v_ref):
    r_ref[...] = _mm(xr_ref[...], wr_ref[...])
    k_ref[...] = _mm(xk_ref[...], wk_ref[...])
    v_ref[...] = _mm(xv_ref[...], wv_ref[...])


def _rkv_proj(xr, xk, xv, wr, wk, wv, layer):
    m, d = xr.shape
    xs = pl.BlockSpec((TM, d), lambda n, i: (i, 0))
    ws = pl.BlockSpec((None, d, TN), lambda n, i: (layer, 0, n))
    os = pl.BlockSpec((TM, TN), lambda n, i: (i, n))
    return pl.pallas_call(
        _rkv_kernel,
        grid=(d // TN, m // TM),
        in_specs=[xs, xs, xs, ws, ws, ws],
        out_specs=[os, os, os],
        out_shape=[jax.ShapeDtypeStruct((m, d), F32)] * 3,
        compiler_params=_cparams("parallel", "arbitrary"),
        name="rkv_proj",
    )(xr, xk, xv, wr, wk, wv)


class _ChunkMasks:
    def __init__(self, sample):
        row = lax.broadcasted_iota(jnp.int32, (CHUNK, PAIR), 0)
        lane = lax.broadcasted_iota(jnp.int32, (CHUNK, PAIR), 1)
        col = lane & (CHUNK - 1)
        r2 = lax.broadcasted_iota(jnp.int32, (CHUNK, CHUNK), 0)
        c2 = lax.broadcasted_iota(jnp.int32, (CHUNK, CHUNK), 1)
        if sample:
            shift = int(math.log2(SEQ_PER_CHUNK))
            seq_mask = SEQ_PER_CHUNK - 1
            same = (row & seq_mask) == (col & seq_mask)
            self.strict = same & ((col >> shift) < (row >> shift))
            self.incl = same & ((col >> shift) <= (row >> shift))
            tri = ((r2 & seq_mask) == (c2 & seq_mask)) & ((c2 >> shift) <= (r2 >> shift))
            self.squarings = 1
        else:
            self.strict = col < row
            self.incl = col <= row
            tri = c2 <= r2
            self.squarings = 5
        self.sample = sample
        self.eye = jnp.where(row == col, 1.0, 0.0).astype(F32)
        self.tri = jnp.where(tri, 1.0, 0.0).astype(BF16)
        self.head0 = lane < HEAD
        r128 = lax.broadcasted_iota(jnp.int32, (PAIR, PAIR), 0)
        c128 = lax.broadcasted_iota(jnp.int32, (PAIR, PAIR), 1)
        self.same_head = (r128 < HEAD) == (c128 < HEAD)
        self.diag128 = r128 == c128


def _blk(y, cm):
    return jnp.concatenate([jnp.where(cm.head0, y, 0.0), jnp.where(cm.head0, 0.0, y)], axis=0).astype(BF16)


def _headsum(xs, cm):
    parts = [(jnp.sum(jnp.where(cm.head0, x, 0.0), axis=1, keepdims=True),
              jnp.sum(jnp.where(cm.head0, 0.0, x), axis=1, keepdims=True)) for x in xs]
    return [jnp.where(cm.head0, s0, s1) for s0, s1 in parts]


def _wkv_pre(kraw, v, ag, vf, vg, kkw, ka, cm):
    kkv = [k * w for k, w in zip(kraw, kkw)]
    sumsq = _headsum([x * x for x in kkv], cm)
    kkn = [x / jnp.maximum(jnp.sqrt(s), 1e-12) for x, s in zip(kkv, sumsq)]
    kmod = [k * (1.0 + (g - 1.0) * c) for k, g, c in zip(kraw, ag, ka)]
    if vf is not None:
        v = [x + (f - x) * g for x, f, g in zip(v, vf, vg)]
    return kmod, v, [-x for x in kkn], [x * g for x, g in zip(kkn, ag)]


def _wkv_post(y, r, kmod, v, g, rk, gng, gnb, cm):
    inv_n = 1.0 / HEAD
    n = len(y)
    sums = _headsum(list(y) + [a * b * c for a, b, c in zip(r, kmod, rk)], cm)
    yc = [a - s * inv_n for a, s in zip(y, sums[:n])]
    var = _headsum([c * c for c in yc], cm)
    return [((c * lax.rsqrt(s * inv_n + GN_EPS) * gg + gb + bs * vv) * gt).astype(BF16)
            for c, s, gg, gb, bs, vv, gt in zip(yc, var, gng, gnb, sums[n:], v, g)]


def _chunk_prep(r, lw, k, v, a, b, cm):
    splits = [_split2(x) for x in lw]
    cum = [_mm(cm.tri, hi) + _mm(cm.tri, lo) for hi, lo in splits]
    if cm.sample:
        cl = [jnp.concatenate([c[CHUNK - SEQ_PER_CHUNK:, :]] * (CHUNK // SEQ_PER_CHUNK), axis=0) for c in cum]
    else:
        cl = [jnp.broadcast_to(c[CHUNK - 1:CHUNK, :], c.shape) for c in cum]
    w_inv = [jnp.exp(-c) for c in cum]
    w_last = [jnp.exp(c) for c in cl]
    rt = [x * jnp.exp(c) for x, c in zip(r, cum)]
    at = [x * jnp.exp(c - l) for x, c, l in zip(a, cum, lw)]
    bt = [x * w for x, w in zip(b, w_inv)]
    kt = [x * w for x, w in zip(k, w_inv)]
    x = [jnp.concatenate([p, q], axis=0).astype(BF16) for p, q in zip(at, rt)]
    blk_bk = [jnp.concatenate([_blk(p, cm), _blk(q, cm)], axis=0) for p, q in zip(bt, kt)]
    prod = [_mm_nt(p, q) for p, q in zip(x, blk_bk)]
    nak = [jnp.where(cm.strict, y[:CHUNK, PAIR:], 0.0).astype(BF16) for y in prod]
    mrbk = [jnp.concatenate([jnp.where(cm.incl, y[CHUNK:, :PAIR], 0.0),
                             jnp.where(cm.incl, y[CHUNK:, PAIR:], 0.0)], axis=1).astype(BF16) for y in prod]
    p = [jnp.where(cm.strict, y[:CHUNK, :PAIR], 0.0) for y in prod]
    t = [cm.eye + y for y in p]
    p = [_mm(y.astype(BF16), _blk(y, cm)) for y in p]
    for _ in range(cm.squarings - 1):
        both = [_mm(jnp.concatenate([y, z], axis=0).astype(BF16), _blk(y, cm)) for y, z in zip(p, t)]
        p = [y[:CHUNK] for y in both]
        t = [z + y[CHUNK:] for z, y in zip(t, both)]
    t = [z + _mm(z.astype(BF16), _blk(y, cm)) for z, y in zip(t, p)]
    tb = [y.astype(BF16) for y in t]
    blk_v = [_blk(y, cm) for y in v]
    q = [_mm(m, w) for m, w in zip(nak, blk_v)]
    au = [_mm(m, jnp.concatenate([_blk(y, cm), _blk(z, cm)], axis=1)) for m, y, z in zip(tb, at, q)]
    ah = [y[:, :PAIR] for y in au]
    uh = [y[:, PAIR:] for y in au]
    rh = [y + _mm(m[:, :PAIR], _blk(z, cm)) for y, m, z in zip(rt, mrbk, ah)]
    yh = [_mm(m, jnp.concatenate([_blk(z, cm), w], axis=0)) for m, z, w in zip(mrbk, uh, blk_v)]
    bh = [y * w for y, w in zip(bt, w_last)]
    kh = [y * w for y, w in zip(kt, w_last)]
    return ah, uh, rh, yh, bh, kh, w_last


def _wkv_prompt_kernel(has_vres, n_chunks, n_pairs, *refs):
    if has_vres:
        (r_ref, k_ref, v_ref, lw_ref, ag_ref, g_ref, vf_ref, vg_ref,
         kkw_ref, ka_ref, rk_ref, gng_ref, gnb_ref, z_ref, s_ref, s_scr) = refs
    else:
        (r_ref, k_ref, v_ref, lw_ref, ag_ref, g_ref,
         kkw_ref, ka_ref, rk_ref, gng_ref, gnb_ref, z_ref, s_ref, s_scr) = refs
    step = pl.program_id(1)

    @pl.when(step == 0)
    def _():
        s_scr[...] = jnp.zeros_like(s_scr)

    cm = _ChunkMasks(sample=False)
    lanes = [slice(p * PAIR, (p + 1) * PAIR) for p in range(n_pairs)]

    def chunk(c, carry):
        rows = pl.ds(pl.multiple_of(c * CHUNK, CHUNK), CHUNK)

        def tok(ref):
            return [ref[rows, l] for l in lanes]

        def chan(ref):
            return [ref[:, l] for l in lanes]

        r = tok(r_ref)
        kmod, v, a, b = _wkv_pre(tok(k_ref), tok(v_ref), tok(ag_ref),
                                 tok(vf_ref) if has_vres else None, tok(vg_ref) if has_vres else None,
                                 chan(kkw_ref), chan(ka_ref), cm)
        ah, uh, rh, yh, bh, kh, w_last = _chunk_prep(r, tok(lw_ref), kmod, v, a, b, cm)
        s16 = [s_scr[p].astype(BF16) for p in range(n_pairs)]
        y = [_mm_nt(x.astype(BF16), h) + z for x, h, z in zip(rh, s16, yh)]
        bh16 = [x.astype(BF16) for x in bh]
        ab = [_mm_tn(x.astype(BF16), w) for x, w in zip(ah, bh16)]
        uvbk = [_mm_tn(jnp.concatenate([p, q], axis=0).astype(BF16),
                       jnp.concatenate([w, x.astype(BF16)], axis=0))
                for p, q, w, x in zip(uh, v, bh16, kh)]
        phi = [(jnp.where(cm.diag128, jnp.broadcast_to(w[:1, :], (PAIR, PAIR)), 0.0)
                + jnp.where(cm.same_head, m, 0.0)).astype(BF16) for w, m in zip(w_last, ab)]
        s_new = [_mm(h, f) + jnp.where(cm.same_head, m, 0.0) for h, f, m in zip(s16, phi, uvbk)]
        for p in range(n_pairs):
            s_scr[p] = s_new[p]
        z = _wkv_post(y, r, kmod, v, tok(g_ref), chan(rk_ref), chan(gng_ref), chan(gnb_ref), cm)
        for l, zz in zip(lanes, z):
            z_ref[rows, l] = zz
        return carry

    lax.fori_loop(0, n_chunks, chunk, 0)

    @pl.when(step == pl.num_programs(1) - 1)
    def _():
        for p in range(n_pairs):
            s = s_scr[p]
            s_ref[0, 2 * p] = s[:HEAD, :HEAD]
            s_ref[0, 2 * p + 1] = s[HEAD:, HEAD:]


def _wkv_prompt(tok, vres, chan, batch, seq):
    m, d = tok[0].shape
    has_vres = vres is not None
    assert d == WKV_PAIRS * PAIR
    steps = seq // WKV_ROWS
    tile = pl.BlockSpec((WKV_ROWS, d), lambda b, s: (b * steps + s, 0))
    args = tuple(tok) + (tuple(vres) if has_vres else ()) + tuple(chan)
    n_tok = len(tok) + (2 if has_vres else 0)
    n_heads = d // HEAD
    return pl.pallas_call(
        functools.partial(_wkv_prompt_kernel, has_vres, WKV_ROWS // CHUNK, WKV_PAIRS),
        grid=(batch, steps),
        in_specs=[tile] * n_tok + [_row(c) for c in chan],
        out_specs=[tile, pl.BlockSpec((1, n_heads, HEAD, HEAD), lambda b, s: (b, 0, 0, 0))],
        out_shape=[jax.ShapeDtypeStruct((m, d), BF16),
                   jax.ShapeDtypeStruct((batch, n_heads, HEAD, HEAD), F32)],
        scratch_shapes=[pltpu.VMEM((WKV_PAIRS, PAIR, PAIR), F32)],
        compiler_params=_cparams("parallel", "arbitrary"),
        name="wkv_prompt",
    )(*args)


def _wkv_sample_kernel(has_vres, n_pairs, *refs):
    if has_vres:
        (r_ref, k_ref, v_ref, lw_ref, ag_ref, g_ref, vf_ref, vg_ref,
         kkw_ref, ka_ref, rk_ref, gng_ref, gnb_ref, s_in_ref, z_ref, s_out_ref,
         x_scr, u_scr, bk_scr, gy_scr) = refs
    else:
        (r_ref, k_ref, v_ref, lw_ref, ag_ref, g_ref,
         kkw_ref, ka_ref, rk_ref, gng_ref, gnb_ref, s_in_ref, z_ref, s_out_ref,
         x_scr, u_scr, bk_scr, gy_scr) = refs
    cm = _ChunkMasks(sample=True)
    row8 = lax.broadcasted_iota(jnp.int32, (8, PAIR), 0)
    zeros_h = jnp.zeros((HEAD, HEAD), F32)
    steps = CHUNK // SEQ_PER_CHUNK
    lanes = [slice(p * PAIR, (p + 1) * PAIR) for p in range(n_pairs)]

    def tok(ref):
        return [ref[:, l] for l in lanes]

    r = tok(r_ref)
    kmod, v, a, b = _wkv_pre(tok(k_ref), tok(v_ref), tok(ag_ref),
                             tok(vf_ref) if has_vres else None, tok(vg_ref) if has_vres else None,
                             tok(kkw_ref), tok(ka_ref), cm)
    ah, uh, rh, yh, bh, kh, w_last = _chunk_prep(r, tok(lw_ref), kmod, v, a, b, cm)
    for p in range(n_pairs):
        x_scr[p, 0:CHUNK, :] = ah[p]
        x_scr[p, CHUNK:, :] = rh[p]
        u_scr[p, 0:CHUNK, :] = uh[p]
        u_scr[p, CHUNK:, :] = yh[p]
        bk_scr[p, 0:CHUNK, :] = bh[p]
        bk_scr[p, CHUNK:, :] = kh[p]
        gy_scr[p, CHUNK:, :] = v[p]

    seqs = range(SEQ_PER_CHUNK)
    for p in range(n_pairs):
        pick = [pl.ds(i, 2 * steps, stride=SEQ_PER_CHUNK) for i in seqs]
        s = [jnp.concatenate([jnp.concatenate([s_in_ref[i, 2 * p], zeros_h], axis=1),
                              jnp.concatenate([zeros_h, s_in_ref[i, 2 * p + 1]], axis=1)], axis=0) for i in seqs]
        uy = [_mm_nt(x_scr[p, pk, :].astype(BF16), m.astype(BF16)) + u_scr[p, pk, :] for pk, m in zip(pick, s)]
        uv = [jnp.where(row8 < steps, m, gy_scr[p, pk, :]) for pk, m in zip(pick, uy)]
        for pk, m in zip(pick, uy):
            gy_scr[p, pk, :] = m
        upd = [_mm_tn(m.astype(BF16), bk_scr[p, pk, :].astype(BF16)) for pk, m in zip(pick, uv)]
        for i in seqs:
            s_new = s[i] * w_last[p][i:i + 1, :] + jnp.where(cm.same_head, upd[i], 0.0)
            s_out_ref[i, 2 * p] = s_new[:HEAD, :HEAD]
            s_out_ref[i, 2 * p + 1] = s_new[HEAD:, HEAD:]

    y = [gy_scr[p, CHUNK:, :] for p in range(n_pairs)]
    z = _wkv_post(y, r, kmod, v, tok(g_ref), tok(rk_ref), tok(gng_ref), tok(gnb_ref), cm)
    for l, zz in zip(lanes, z):
        z_ref[:, l] = zz


def _wkv_sample(tok, vres, chan, state, layer):
    m, d = tok[0].shape
    nb = state.shape[1]
    has_vres = vres is not None
    n_pairs = WKV_SAMPLE_PAIRS
    width = n_pairs * PAIR
    tile = pl.BlockSpec((CHUNK, width), lambda i, p: (i, p))
    row = pl.BlockSpec((1, width), lambda i, p: (0, p))
    blk = (SEQ_PER_CHUNK, 2 * n_pairs, HEAD, HEAD)
    args = tuple(tok) + (tuple(vres) if has_vres else ()) + tuple(chan) + (state,)
    n_tok = len(tok) + (2 if has_vres else 0)
    return pl.pallas_call(
        functools.partial(_wkv_sample_kernel, has_vres, n_pairs),
        grid=(nb // SEQ_PER_CHUNK, d // width),
        in_specs=[tile] * n_tok + [row] * len(chan)
        + [pl.BlockSpec((None,) + blk, lambda i, p: (layer, i, p, 0, 0))],
        out_specs=[tile, pl.BlockSpec(blk, lambda i, p: (i, p, 0, 0))],
        out_shape=[jax.ShapeDtypeStruct((m, d), BF16), jax.ShapeDtypeStruct(state.shape[1:], F32)],
        scratch_shapes=[pltpu.VMEM((n_pairs, 2 * CHUNK, PAIR), F32)] * 4,
        compiler_params=_cparams("parallel", "parallel"),
        name="wkv_sample",
    )(*args)


def _out_ln_kernel(alpha, z_ref, x_ref, w_ref, g_ref, b_ref, o_ref, ob_ref):
    y = _layer_norm(alpha * x_ref[...] + _mm(z_ref[...], w_ref[...]), g_ref[...], b_ref[...])
    o_ref[...] = y
    ob_ref[...] = y.astype(BF16)


def _out_ln(alpha, z, x, w, layer, g, b):
    m, d = x.shape
    tile = pl.BlockSpec((TM, d), lambda i: (i, 0))
    return pl.pallas_call(
        functools.partial(_out_ln_kernel, alpha),
        grid=(m // TM,),
        in_specs=[tile, tile, pl.BlockSpec((None, d, d), lambda i: (layer, 0, 0)), _row(g), _row(b)],
        out_specs=[tile, tile],
        out_shape=[jax.ShapeDtypeStruct((m, d), F32), jax.ShapeDtypeStruct((m, d), BF16)],
        compiler_params=_cparams("parallel"),
        name="out_ln",
    )(z, x, w, g, b)


def _ffn_kernel(alpha, xb_ref, x_ref, wg_ref, wu_ref, wd_ref, g_ref, b_ref, o_ref, ob_ref, acc_ref):
    f = pl.program_id(1)

    @pl.when(f == 0)
    def _():
        acc_ref[...] = jnp.zeros_like(acc_ref)

    xb = xb_ref[...]
    gate = _mm(xb, wg_ref[...])
    up = _mm(xb, wu_ref[...])
    act = (gate * _sigmoid(gate) * up).astype(BF16)
    for c in range(0, acc_ref.shape[1], FFN_DOWN_CHUNK):
        cols = slice(c, c + FFN_DOWN_CHUNK)
        acc_ref[:, cols] += _mm(act, wd_ref[:, cols])

    @pl.when(f == pl.num_programs(1) - 1)
    def _():
        y = _layer_norm(alpha * x_ref[...] + acc_ref[...], g_ref[...], b_ref[...])
        o_ref[...] = y
        ob_ref[...] = y.astype(BF16)


def _ffn(alpha, xb, x, w_in, w_down, layer, g, b):
    m, d = x.shape
    d_ff = w_down.shape[1]
    nf = d_ff // TF
    tile = pl.BlockSpec((TM, d), lambda i, f: (i, 0))
    return pl.pallas_call(
        functools.partial(_ffn_kernel, alpha),
        grid=(m // TM, nf),
        in_specs=[tile, tile,
                  pl.BlockSpec((None, d, TF), lambda i, f: (layer, 0, f)),
                  pl.BlockSpec((None, d, TF), lambda i, f: (layer, 0, f + nf)),
                  pl.BlockSpec((None, TF, d), lambda i, f: (layer, f, 0)),
                  _row(g), _row(b)],
        out_specs=[tile, tile],
        out_shape=[jax.ShapeDtypeStruct((m, d), F32), jax.ShapeDtypeStruct((m, d), BF16)],
        scratch_shapes=[pltpu.VMEM((TM, d), F32)],
        compiler_params=_cparams("parallel", "arbitrary"),
        name="ffn",
    )(xb, x, w_in, w_in, w_down, g, b)


def _pool_prompt_kernel(alpha, tiles_per_seq, x_ref, halo_ref, w_ref, sc_ref, g_ref, b_ref, o_ref, ob_ref):
    tile_in_seq = pl.program_id(0) % tiles_per_seq
    x = x_ref[...]
    halo = jnp.where(tile_in_seq == 0, 0.0, halo_ref[...])
    gw = x.shape[1] // len(POOL_WINDOWS)
    pos = tile_in_seq * TM + lax.broadcasted_iota(jnp.int32, (TM, gw), 0)
    outs = []
    for gi, win in enumerate(POOL_WINDOWS):
        lanes = slice(gi * gw, (gi + 1) * gw)
        xg = x[:, lanes]
        s = jnp.concatenate([halo[:, lanes], xg], axis=0)
        span = 1
        while span < win:
            s = s[span:] + s[:-span]
            span *= 2
        first = HALO - (win - 1)
        cnt = jnp.minimum(win, pos + 1).astype(F32)
        dg = s[first:first + TM] / cnt - xg
        outs.append(_mm(dg.astype(BF16), w_ref[gi]))
    h = jnp.concatenate(outs, axis=1) * sc_ref[...]
    y = _layer_norm(alpha * x + h, g_ref[...], b_ref[...])
    o_ref[...] = y
    ob_ref[...] = y.astype(BF16)


def _pool_prompt(alpha, x, w, layer, scale, g, b, seq):
    m, d = x.shape
    tile = pl.BlockSpec((TM, d), lambda i: (i, 0))
    halo = pl.BlockSpec((HALO, d), lambda i: (jnp.maximum(i * (TM // HALO) - 1, 0), 0))
    return pl.pallas_call(
        functools.partial(_pool_prompt_kernel, alpha, seq // TM),
        grid=(m // TM,),
        in_specs=[tile, halo, pl.BlockSpec((None,) + w.shape[1:], lambda i: (layer, 0, 0, 0)),
                  _row(scale), _row(g), _row(b)],
        out_specs=[tile, tile],
        out_shape=[jax.ShapeDtypeStruct((m, d), F32), jax.ShapeDtypeStruct((m, d), BF16)],
        compiler_params=_cparams("parallel"),
        name="pool_prompt",
    )(x, x, w, scale, g, b)


def _pool_sample_kernel(ext_ref, w_ref, sc_ref, h_ref):
    steps = ext_ref.shape[0] - POOL_BUF
    gi = pl.program_id(0)
    ds = []
    for t in range(steps):
        cur = ext_ref[POOL_BUF + t]
        acc16 = cur
        sums = {}
        for i in range(1, max(POOL_WINDOWS)):
            acc16 = acc16 + ext_ref[POOL_BUF + t - i]
            if i + 1 in POOL_WINDOWS:
                sums[i + 1] = acc16
        d = sums[POOL_WINDOWS[-1]] * (1.0 / POOL_WINDOWS[-1])
        for j, win in enumerate(POOL_WINDOWS[:-1]):
            d = jnp.where(gi == j, sums[win] * (1.0 / win), d)
        ds.append((d - cur).astype(BF16))
    h_ref[...] = _mm(jnp.concatenate(ds, axis=0), w_ref[...]) * sc_ref[...]


def _pool_sample(ext, w, layer, scale):
    n, nb, d = ext.shape
    ng = len(POOL_WINDOWS)
    gw = d // ng
    return pl.pallas_call(
        _pool_sample_kernel,
        grid=(ng,),
        in_specs=[pl.BlockSpec((n, nb, gw), lambda gi: (0, 0, gi)),
                  pl.BlockSpec((None, None, gw, gw), lambda gi: (layer, gi, 0, 0)),
                  pl.BlockSpec((1, gw), lambda gi: (0, gi))],
        out_specs=pl.BlockSpec(((n - POOL_BUF) * nb, gw), lambda gi: (0, gi)),
        out_shape=jax.ShapeDtypeStruct(((n - POOL_BUF) * nb, d), F32),
        compiler_params=_cparams("parallel"),
        name="pool_sample",
    )(ext, w, scale)


def _add_ln_kernel(alpha, x_ref, h_ref, g_ref, b_ref, o_ref, ob_ref):
    y = _layer_norm(alpha * x_ref[...] + h_ref[...], g_ref[...], b_ref[...])
    o_ref[...] = y
    ob_ref[...] = y.astype(BF16)


def _add_ln(alpha, x, h, g, b):
    m, d = x.shape
    tile = pl.BlockSpec((TM, d), lambda i: (i, 0))
    return pl.pallas_call(
        functools.partial(_add_ln_kernel, alpha),
        grid=(m // TM,),
        in_specs=[tile, tile, _row(g), _row(b)],
        out_specs=[tile, tile],
        out_shape=[jax.ShapeDtypeStruct((m, d), F32), jax.ShapeDtypeStruct((m, d), BF16)],
        compiler_params=_cparams("parallel"),
        name="add_ln",
    )(x, h, g, b)


def _pad_lora(w_a, w_b):
    rank = w_a.shape[1]
    pad = (-rank) % LORA_PAD
    return (jnp.pad(w_a, ((0, 0), (0, pad))).astype(BF16), jnp.pad(w_b, ((0, pad), (0, 0))).astype(BF16))


def kernel(x_prompt, x_sample, state_wkv, state_shift, state_pool, ln_g, ln_b, rw_mu, rw_wr, rw_wk, rw_wv, rw_wo, rw_w0, rw_w1, rw_w2, rw_a0, rw_a1, rw_a2, rw_v0, rw_v1, rw_v2, rw_g1, rw_g2, rw_kk, rw_ka, rw_rk, rw_gn_g, rw_gn_b, pool_w, pool_scale, ffn_w_in, ffn_w_down):
    bp, seq, d = x_prompt.shape
    bs, steps, _ = x_sample.shape
    depth = ln_g.shape[0]
    n_mixers = 2
    alpha = float((2 * depth) ** 0.25)
    m_sample = bs * steps
    nblk = bs // SEQ_PER_CHUNK
    assert steps * SEQ_PER_CHUNK == CHUNK and bs % SEQ_PER_CHUNK == 0
    assert seq % WKV_ROWS == 0 and seq % TM == 0 and m_sample % TM == 0

    def sample_to_rows(a):
        return a.reshape(nblk, SEQ_PER_CHUNK, steps, d).transpose(0, 2, 1, 3).reshape(m_sample, d)

    def rows_to_sample(a):
        return a.reshape(nblk, steps, SEQ_PER_CHUNK, d).transpose(0, 2, 1, 3).reshape(bs, steps, d)

    wr, wk, wv, wo = (w.astype(BF16) for w in (rw_wr, rw_wk, rw_wv, rw_wo))
    w_in, w_down, w_pool = ffn_w_in.astype(BF16), ffn_w_down.astype(BF16), pool_w.astype(BF16)

    xp = x_prompt.reshape(bp * seq, d)
    xs = sample_to_rows(x_sample)
    xpb = xsb = None
    vf_p = vf_s = None
    new_wkv_p, new_wkv_s, new_shift_p, new_shift_s, new_pool_p, sample_pool_rows = [], [], [], [], [], []
    for i in range(depth):
        j = i // n_mixers
        lg, lb = ln_g[i, 0][None, :], ln_b[i, 0][None, :]
        xs4 = xs.reshape(nblk, steps, SEQ_PER_CHUNK, d)
        if i % n_mixers == 0:
            prev_s = jnp.concatenate([state_shift[j].reshape(nblk, 1, SEQ_PER_CHUNK, d), xs4[:, :-1]],
                                     axis=1).reshape(m_sample, d)
            new_shift_p.append(jnp.concatenate([xp[(b + 1) * seq - 1:(b + 1) * seq] for b in range(bp)]))
            new_shift_s.append(xs4[:, -1].reshape(bs, d))

            w1, w2 = _pad_lora(rw_w1[j], rw_w2[j])
            a1, a2 = _pad_lora(rw_a1[j], rw_a2[j])
            g1, g2 = rw_g1[j].astype(BF16), rw_g2[j].astype(BF16)
            if j == 0:
                params = (rw_mu[j], rw_w0[j][None, :], rw_a0[j][None, :], w1, w2, a1, a2, g1, g2)
            else:
                v1, v2 = _pad_lora(rw_v1[j - 1], rw_v2[j - 1])
                params = (rw_mu[j], rw_w0[j][None, :], rw_a0[j][None, :], rw_v0[j - 1][None, :],
                          w1, w2, a1, a2, v1, v2, g1, g2)
            chan = (rw_kk[j][None, :], rw_ka[j][None, :], rw_rk[j].reshape(1, d),
                    rw_gn_g[j][None, :], rw_gn_b[j][None, :])

            def time_mix(x, xprev):
                mixed = _rwkv_mix(x, xprev, seq, params)
                xr, xk, xv, lw, ag, g = mixed[:6]
                r, k, v = _rkv_proj(xr, xk, xv, wr, wk, wv, j)
                return (r, k, v, lw, ag, g), (mixed[6] if j > 0 else None)

            tok_p, vg_p = time_mix(xp, None)
            tok_s, vg_s = time_mix(xs, prev_s)
            if j == 0:
                vf_p, vf_s = tok_p[2], tok_s[2]
            zp, s_p = _wkv_prompt(tok_p, None if j == 0 else (vf_p, vg_p), chan, bp, seq)
            zs, s_s = _wkv_sample(tok_s, None if j == 0 else (vf_s, vg_s), chan, state_wkv, j)
            new_wkv_p.append(s_p)
            new_wkv_s.append(s_s)
            xp, xpb = _out_ln(alpha, zp, xp, wo, j, lg, lb)
            xs, xsb = _out_ln(alpha, zs, xs, wo, j, lg, lb)
        else:
            xs_tb = xs4.transpose(1, 0, 2, 3).reshape(steps, bs, d)
            new_pool_p.append(jnp.stack([xp[(b + 1) * seq - POOL_BUF:(b + 1) * seq] for b in range(bp)]))
            sample_pool_rows.append(xs_tb.transpose(1, 0, 2))
            sc = pool_scale[j][None, :]
            ext = jnp.concatenate([state_pool[j].transpose(1, 0, 2), xs_tb], axis=0)
            h_tb = _pool_sample(ext, w_pool, j, sc)
            h_rows = h_tb.reshape(steps, nblk, SEQ_PER_CHUNK, d).transpose(1, 0, 2, 3).reshape(m_sample, d)
            xp, xpb = _pool_prompt(alpha, xp, w_pool, j, sc, lg, lb, seq)
            xs, xsb = _add_ln(alpha, xs, h_rows, lg, lb)
        lg, lb = ln_g[i, 1][None, :], ln_b[i, 1][None, :]
        xp, xpb = _ffn(alpha, xpb, xp, w_in, w_down, i, lg, lb)
        xs, xsb = _ffn(alpha, xsb, xs, w_in, w_down, i, lg, lb)

    new_pool_s = jnp.concatenate([state_pool[:, :, steps:], jnp.stack(sample_pool_rows)], axis=2)
    return (xp.reshape(bp, seq, d), rows_to_sample(xs), jnp.stack(new_wkv_p), jnp.stack(new_shift_p),
            jnp.stack(new_pool_p), jnp.stack(new_wkv_s), jnp.stack(new_shift_s), new_pool_s)
```

```python
import functools
import math

import jax
import jax.numpy as jnp
from jax import lax
from jax.experimental import pallas as pl
from jax.experimental.pallas import tpu as pltpu

F32 = jnp.float32
BF16 = jnp.bfloat16

HEAD = 64
PAIR = 2 * HEAD
CHUNK = 64
SEQ_PER_CHUNK = 16
GN_EPS = 64e-5
LN_EPS = 1e-5
POOL_WINDOWS = (2, 4, 8, 16)
POOL_BUF = 15
HALO = 16
SHIFT_HALO = 8
LORA_PAD = 128

TM = 512
TM_MIX = 256
TN = 512
TN_CAST = 256
TF = 512
TF_CAST = 256
FFN_DOWN_CHUNK = 512
WKV_ROWS = 128
WKV_PAIRS = 16
WKV_SAMPLE_PAIRS = 4
VMEM_LIMIT = 56 * 1024 * 1024


def _cparams(*sem):
    return pltpu.CompilerParams(dimension_semantics=sem, vmem_limit_bytes=VMEM_LIMIT)


def _mm(a, b):
    return jnp.dot(a, b, preferred_element_type=F32)


def _mm_nt(a, b):
    return lax.dot_general(a, b, (((1,), (1,)), ((), ())), preferred_element_type=F32)


def _mm_tn(a, b):
    return lax.dot_general(a, b, (((0,), (0,)), ((), ())), preferred_element_type=F32)


def _sigmoid(x):
    return 1.0 / (1.0 + jnp.exp(-x))


def _layer_norm(v, g, b):
    mu = jnp.mean(v, axis=-1, keepdims=True)
    c = v - mu
    var = jnp.mean(c * c, axis=-1, keepdims=True)
    return c * lax.rsqrt(var + LN_EPS) * g + b


def _split2(x):
    hi = x.astype(BF16)
    return hi, (x - hi.astype(F32)).astype(BF16)


def _row(a):
    return pl.BlockSpec((1, a.shape[-1]), lambda *_: (0, 0))


def _mix_kernel(has_vres, tiles_per_seq, *refs):
    if has_vres:
        (x_ref, prev_ref, mu_ref, w0_ref, a0_ref, v0_ref, w1_ref, w2_ref, a1_ref, a2_ref,
         v1_ref, v2_ref, g1_ref, g2_ref, xr_ref, xk_ref, xv_ref, lw_ref, ag_ref, g_ref, vg_ref) = refs
    else:
        (x_ref, prev_ref, mu_ref, w0_ref, a0_ref, w1_ref, w2_ref, a1_ref, a2_ref,
         g1_ref, g2_ref, xr_ref, xk_ref, xv_ref, lw_ref, ag_ref, g_ref) = refs
    x = x_ref[...]
    if tiles_per_seq is None:
        xprev = prev_ref[...]
    else:
        before = jnp.where(pl.program_id(0) % tiles_per_seq == 0, 0.0, prev_ref[SHIFT_HALO - 1:SHIFT_HALO, :])
        first_row = lax.broadcasted_iota(jnp.int32, x.shape, 0) == 0
        xprev = jnp.where(first_row, before, pltpu.roll(x, 1, 0))
    xx = xprev - x

    def mix(j):
        return (x + xx * mu_ref[j:j + 1, :]).astype(BF16)

    xr_ref[...] = mix(0)
    xk_ref[...] = mix(2)
    xv = mix(3)
    xv_ref[...] = xv

    u = w0_ref[...] + _mm(jnp.tanh(_mm(mix(1), w1_ref[...])).astype(BF16), w2_ref[...])
    z = -u
    softplus = jnp.maximum(z, 0.0) + jnp.log(1.0 + jnp.exp(-jnp.abs(z)))
    lw_ref[...] = -jnp.exp(-softplus - 0.5)
    ag_ref[...] = _sigmoid(a0_ref[...] + _mm(_mm(mix(4), a1_ref[...]).astype(BF16), a2_ref[...]))
    g_ref[...] = _mm(_sigmoid(_mm(mix(5), g1_ref[...])).astype(BF16), g2_ref[...])
    if has_vres:
        vg_ref[...] = _sigmoid(v0_ref[...] + _mm(_mm(xv, v1_ref[...]).astype(BF16), v2_ref[...]))


def _rwkv_mix(x, xprev, seq, params):
    m, d = x.shape
    has_vres = len(params) == 12
    tile = pl.BlockSpec((TM_MIX, d), lambda i: (i, 0))
    if xprev is None:
        prev, prev_spec = x, pl.BlockSpec(
            (SHIFT_HALO, d), lambda i: (jnp.maximum(i * (TM_MIX // SHIFT_HALO) - 1, 0), 0))
        tiles_per_seq = seq // TM_MIX
    else:
        prev, prev_spec, tiles_per_seq = xprev, tile, None

    def full(a):
        return pl.BlockSpec(a.shape, lambda i: (0,) * a.ndim)

    n_f32_out = 4 if has_vres else 3
    out_shape = [jax.ShapeDtypeStruct((m, d), BF16)] * 3 + [jax.ShapeDtypeStruct((m, d), F32)] * n_f32_out
    return pl.pallas_call(
        functools.partial(_mix_kernel, has_vres, tiles_per_seq),
        grid=(m // TM_MIX,),
        in_specs=[tile, prev_spec] + [full(a) for a in params],
        out_specs=[tile] * len(out_shape),
        out_shape=out_shape,
        compiler_params=_cparams("parallel"),
        name="rwkv_mix",
    )(x, prev, *params)


def _rkv_kernel(xr_ref, xk_ref, xv_ref, wr_ref, wk_ref, wv_ref, r_ref, k_ref, v_ref):
    r_ref[...] = _mm(xr_ref[...], wr_ref[...])
    k_ref[...] = _mm(xk_ref[...], wk_ref[...])
    v_ref[...] = _mm(xv_ref[...], wv_ref[...])


def _rkv_proj(xr, xk, xv, wr, wk, wv):
    m, d = xr.shape
    xs = pl.BlockSpec((TM, d), lambda n, i: (i, 0))
    ws = pl.BlockSpec((d, TN), lambda n, i: (0, n))
    os = pl.BlockSpec((TM, TN), lambda n, i: (i, n))
    return pl.pallas_call(
        _rkv_kernel,
        grid=(d // TN, m // TM),
        in_specs=[xs, xs, xs, ws, ws, ws],
        out_specs=[os, os, os],
        out_shape=[jax.ShapeDtypeStruct((m, d), F32)] * 3,
        compiler_params=_cparams("parallel", "arbitrary"),
        name="rkv_proj",
    )(xr, xk, xv, wr, wk, wv)


def _rkv_cast_kernel(xr_ref, xk_ref, xv_ref, wr_ref, wk_ref, wv_ref, r_ref, k_ref, v_ref, wrb_ref, wkb_ref, wvb_ref):
    for x_ref, w_ref, o_ref, wb_ref in ((xr_ref, wr_ref, r_ref, wrb_ref), (xk_ref, wk_ref, k_ref, wkb_ref),
                                        (xv_ref, wv_ref, v_ref, wvb_ref)):
        wb = w_ref[...].astype(BF16)
        wb_ref[...] = wb
        o_ref[...] = _mm(x_ref[...], wb)


def _rkv_proj_cast(xr, xk, xv, wr, wk, wv, layer):
    m, d = xr.shape
    assert m == TM
    xs = pl.BlockSpec((TM, d), lambda n: (0, 0))
    ws = pl.BlockSpec((None, d, TN_CAST), lambda n: (layer, 0, n))
    os = pl.BlockSpec((TM, TN_CAST), lambda n: (0, n))
    wbs = pl.BlockSpec((d, TN_CAST), lambda n: (0, n))
    return pl.pallas_call(
        _rkv_cast_kernel,
        grid=(d // TN_CAST,),
        in_specs=[xs, xs, xs, ws, ws, ws],
        out_specs=[os, os, os, wbs, wbs, wbs],
        out_shape=[jax.ShapeDtypeStruct((m, d), F32)] * 3 + [jax.ShapeDtypeStruct((d, d), BF16)] * 3,
        compiler_params=_cparams("parallel"),
        name="rkv_proj_cast",
    )(xr, xk, xv, wr, wk, wv)


class _ChunkMasks:
    def __init__(self, sample):
        row = lax.broadcasted_iota(jnp.int32, (CHUNK, PAIR), 0)
        lane = lax.broadcasted_iota(jnp.int32, (CHUNK, PAIR), 1)
        col = lane & (CHUNK - 1)
        r2 = lax.broadcasted_iota(jnp.int32, (CHUNK, CHUNK), 0)
        c2 = lax.broadcasted_iota(jnp.int32, (CHUNK, CHUNK), 1)
        if sample:
            shift = int(math.log2(SEQ_PER_CHUNK))
            seq_mask = SEQ_PER_CHUNK - 1
            same = (row & seq_mask) == (col & seq_mask)
            self.strict = same & ((col >> shift) < (row >> shift))
            self.incl = same & ((col >> shift) <= (row >> shift))
            tri = ((r2 & seq_mask) == (c2 & seq_mask)) & ((c2 >> shift) <= (r2 >> shift))
            self.squarings = 1
        else:
            self.strict = col < row
            self.incl = col <= row
            tri = c2 <= r2
            self.squarings = 5
        self.sample = sample
        self.eye = jnp.where(row == col, 1.0, 0.0).astype(F32)
        self.tri = jnp.where(tri, 1.0, 0.0).astype(BF16)
        self.head0 = lane < HEAD
        r128 = lax.broadcasted_iota(jnp.int32, (PAIR, PAIR), 0)
        c128 = lax.broadcasted_iota(jnp.int32, (PAIR, PAIR), 1)
        self.same_head = (r128 < HEAD) == (c128 < HEAD)
        self.diag128 = r128 == c128


def _blk(y, cm):
    return jnp.concatenate([jnp.where(cm.head0, y, 0.0), jnp.where(cm.head0, 0.0, y)], axis=0).astype(BF16)


def _headsum(xs, cm):
    parts = [(jnp.sum(jnp.where(cm.head0, x, 0.0), axis=1, keepdims=True),
              jnp.sum(jnp.where(cm.head0, 0.0, x), axis=1, keepdims=True)) for x in xs]
    return [jnp.where(cm.head0, s0, s1) for s0, s1 in parts]


def _wkv_pre(kraw, v, ag, vf, vg, kkw, ka, cm):
    kkv = [k * w for k, w in zip(kraw, kkw)]
    sumsq = _headsum([x * x for x in kkv], cm)
    kkn = [x / jnp.maximum(jnp.sqrt(s), 1e-12) for x, s in zip(kkv, sumsq)]
    kmod = [k * (1.0 + (g - 1.0) * c) for k, g, c in zip(kraw, ag, ka)]
    if vf is not None:
        v = [x + (f - x) * g for x, f, g in zip(v, vf, vg)]
    return kmod, v, [-x for x in kkn], [x * g for x, g in zip(kkn, ag)]


def _wkv_post(y, r, kmod, v, g, rk, gng, gnb, cm):
    inv_n = 1.0 / HEAD
    n = len(y)
    sums = _headsum(list(y) + [a * b * c for a, b, c in zip(r, kmod, rk)], cm)
    yc = [a - s * inv_n for a, s in zip(y, sums[:n])]
    var = _headsum([c * c for c in yc], cm)
    return [((c * lax.rsqrt(s * inv_n + GN_EPS) * gg + gb + bs * vv) * gt).astype(BF16)
            for c, s, gg, gb, bs, vv, gt in zip(yc, var, gng, gnb, sums[n:], v, g)]


def _chunk_prep(r, lw, k, v, a, b, cm):
    splits = [_split2(x) for x in lw]
    cum = [_mm(cm.tri, hi) + _mm(cm.tri, lo) for hi, lo in splits]
    if cm.sample:
        cl = [jnp.concatenate([c[CHUNK - SEQ_PER_CHUNK:, :]] * (CHUNK // SEQ_PER_CHUNK), axis=0) for c in cum]
    else:
        cl = [jnp.broadcast_to(c[CHUNK - 1:CHUNK, :], c.shape) for c in cum]
    w_inv = [jnp.exp(-c) for c in cum]
    w_last = [jnp.exp(c) for c in cl]
    rt = [x * jnp.exp(c) for x, c in zip(r, cum)]
    at = [x * jnp.exp(c - l) for x, c, l in zip(a, cum, lw)]
    bt = [x * w for x, w in zip(b, w_inv)]
    kt = [x * w for x, w in zip(k, w_inv)]
    x = [jnp.concatenate([p, q], axis=0).astype(BF16) for p, q in zip(at, rt)]
    blk_bk = [jnp.concatenate([_blk(p, cm), _blk(q, cm)], axis=0) for p, q in zip(bt, kt)]
    prod = [_mm_nt(p, q) for p, q in zip(x, blk_bk)]
    nak = [jnp.where(cm.strict, y[:CHUNK, PAIR:], 0.0).astype(BF16) for y in prod]
    mrbk = [jnp.concatenate([jnp.where(cm.incl, y[CHUNK:, :PAIR], 0.0),
                             jnp.where(cm.incl, y[CHUNK:, PAIR:], 0.0)], axis=1).astype(BF16) for y in prod]
    p = [jnp.where(cm.strict, y[:CHUNK, :PAIR], 0.0) for y in prod]
    t = [cm.eye + y for y in p]
    p = [_mm(y.astype(BF16), _blk(y, cm)) for y in p]
    for _ in range(cm.squarings - 1):
        both = [_mm(jnp.concatenate([y, z], axis=0).astype(BF16), _blk(y, cm)) for y, z in zip(p, t)]
        p = [y[:CHUNK] for y in both]
        t = [z + y[CHUNK:] for z, y in zip(t, both)]
    t = [z + _mm(z.astype(BF16), _blk(y, cm)) for z, y in zip(t, p)]
    tb = [y.astype(BF16) for y in t]
    blk_v = [_blk(y, cm) for y in v]
    q = [_mm(m, w) for m, w in zip(nak, blk_v)]
    au = [_mm(m, jnp.concatenate([_blk(y, cm), _blk(z, cm)], axis=1)) for m, y, z in zip(tb, at, q)]
    ah = [y[:, :PAIR] for y in au]
    uh = [y[:, PAIR:] for y in au]
    rh = [y + _mm(m[:, :PAIR], _blk(z, cm)) for y, m, z in zip(rt, mrbk, ah)]
    yh = [_mm(m, jnp.concatenate([_blk(z, cm), w], axis=0)) for m, z, w in zip(mrbk, uh, blk_v)]
    bh = [y * w for y, w in zip(bt, w_last)]
    kh = [y * w for y, w in zip(kt, w_last)]
    return ah, uh, rh, yh, bh, kh, w_last


def _wkv_prompt_kernel(has_vres, n_chunks, n_pairs, *refs):
    if has_vres:
        (r_ref, k_ref, v_ref, lw_ref, ag_ref, g_ref, vf_ref, vg_ref,
         kkw_ref, ka_ref, rk_ref, gng_ref, gnb_ref, z_ref, s_ref, s_scr) = refs
    else:
        (r_ref, k_ref, v_ref, lw_ref, ag_ref, g_ref,
         kkw_ref, ka_ref, rk_ref, gng_ref, gnb_ref, z_ref, s_ref, s_scr) = refs
    step = pl.program_id(1)

    @pl.when(step == 0)
    def _():
        s_scr[...] = jnp.zeros_like(s_scr)

    cm = _ChunkMasks(sample=False)
    lanes = [slice(p * PAIR, (p + 1) * PAIR) for p in range(n_pairs)]

    def chunk(c, carry):
        rows = pl.ds(pl.multiple_of(c * CHUNK, CHUNK), CHUNK)

        def tok(ref):
            return [ref[rows, l] for l in lanes]

        def chan(ref):
            return [ref[:, l] for l in lanes]

        r = tok(r_ref)
        kmod, v, a, b = _wkv_pre(tok(k_ref), tok(v_ref), tok(ag_ref),
                                 tok(vf_ref) if has_vres else None, tok(vg_ref) if has_vres else None,
                                 chan(kkw_ref), chan(ka_ref), cm)
        ah, uh, rh, yh, bh, kh, w_last = _chunk_prep(r, tok(lw_ref), kmod, v, a, b, cm)
        s16 = [s_scr[p].astype(BF16) for p in range(n_pairs)]
        y = [_mm_nt(x.astype(BF16), h) + z for x, h, z in zip(rh, s16, yh)]
        bh16 = [x.astype(BF16) for x in bh]
        ab = [_mm_tn(x.astype(BF16), w) for x, w in zip(ah, bh16)]
        uvbk = [_mm_tn(jnp.concatenate([p, q], axis=0).astype(BF16),
                       jnp.concatenate([w, x.astype(BF16)], axis=0))
                for p, q, w, x in zip(uh, v, bh16, kh)]
        phi = [(jnp.where(cm.diag128, jnp.broadcast_to(w[:1, :], (PAIR, PAIR)), 0.0)
                + jnp.where(cm.same_head, m, 0.0)).astype(BF16) for w, m in zip(w_last, ab)]
        s_new = [_mm(h, f) + jnp.where(cm.same_head, m, 0.0) for h, f, m in zip(s16, phi, uvbk)]
        for p in range(n_pairs):
            s_scr[p] = s_new[p]
        z = _wkv_post(y, r, kmod, v, tok(g_ref), chan(rk_ref), chan(gng_ref), chan(gnb_ref), cm)
        for l, zz in zip(lanes, z):
            z_ref[rows, l] = zz
        return carry

    lax.fori_loop(0, n_chunks, chunk, 0)

    @pl.when(step == pl.num_programs(1) - 1)
    def _():
        for p in range(n_pairs):
            s = s_scr[p]
            s_ref[0, 2 * p] = s[:HEAD, :HEAD]
            s_ref[0, 2 * p + 1] = s[HEAD:, HEAD:]


def _wkv_prompt(tok, vres, chan, batch, seq):
    m, d = tok[0].shape
    has_vres = vres is not None
    assert d == WKV_PAIRS * PAIR
    steps = seq // WKV_ROWS
    tile = pl.BlockSpec((WKV_ROWS, d), lambda b, s: (b * steps + s, 0))
    args = tuple(tok) + (tuple(vres) if has_vres else ()) + tuple(chan)
    n_tok = len(tok) + (2 if has_vres else 0)
    n_heads = d // HEAD
    return pl.pallas_call(
        functools.partial(_wkv_prompt_kernel, has_vres, WKV_ROWS // CHUNK, WKV_PAIRS),
        grid=(batch, steps),
        in_specs=[tile] * n_tok + [_row(c) for c in chan],
        out_specs=[tile, pl.BlockSpec((1, n_heads, HEAD, HEAD), lambda b, s: (b, 0, 0, 0))],
        out_shape=[jax.ShapeDtypeStruct((m, d), BF16),
                   jax.ShapeDtypeStruct((batch, n_heads, HEAD, HEAD), F32)],
        scratch_shapes=[pltpu.VMEM((WKV_PAIRS, PAIR, PAIR), F32)],
        compiler_params=_cparams("parallel", "arbitrary"),
        name="wkv_prompt",
    )(*args)


def _wkv_sample_kernel(has_vres, n_pairs, *refs):
    if has_vres:
        (r_ref, k_ref, v_ref, lw_ref, ag_ref, g_ref, vf_ref, vg_ref,
         kkw_ref, ka_ref, rk_ref, gng_ref, gnb_ref, s_in_ref, z_ref, s_out_ref,
         x_scr, u_scr, bk_scr, gy_scr) = refs
    else:
        (r_ref, k_ref, v_ref, lw_ref, ag_ref, g_ref,
         kkw_ref, ka_ref, rk_ref, gng_ref, gnb_ref, s_in_ref, z_ref, s_out_ref,
         x_scr, u_scr, bk_scr, gy_scr) = refs
    cm = _ChunkMasks(sample=True)
    row8 = lax.broadcasted_iota(jnp.int32, (8, PAIR), 0)
    zeros_h = jnp.zeros((HEAD, HEAD), F32)
    steps = CHUNK // SEQ_PER_CHUNK
    lanes = [slice(p * PAIR, (p + 1) * PAIR) for p in range(n_pairs)]

    def tok(ref):
        return [ref[:, l] for l in lanes]

    r = tok(r_ref)
    kmod, v, a, b = _wkv_pre(tok(k_ref), tok(v_ref), tok(ag_ref),
                             tok(vf_ref) if has_vres else None, tok(vg_ref) if has_vres else None,
                             tok(kkw_ref), tok(ka_ref), cm)
    ah, uh, rh, yh, bh, kh, w_last = _chunk_prep(r, tok(lw_ref), kmod, v, a, b, cm)
    for p in range(n_pairs):
        x_scr[p, 0:CHUNK, :] = ah[p]
        x_scr[p, CHUNK:, :] = rh[p]
        u_scr[p, 0:CHUNK, :] = uh[p]
        u_scr[p, CHUNK:, :] = yh[p]
        bk_scr[p, 0:CHUNK, :] = bh[p]
        bk_scr[p, CHUNK:, :] = kh[p]
        gy_scr[p, CHUNK:, :] = v[p]

    seqs = range(SEQ_PER_CHUNK)
    for p in range(n_pairs):
        pick = [pl.ds(i, 2 * steps, stride=SEQ_PER_CHUNK) for i in seqs]
        s = [jnp.concatenate([jnp.concatenate([s_in_ref[i, 2 * p], zeros_h], axis=1),
                              jnp.concatenate([zeros_h, s_in_ref[i, 2 * p + 1]], axis=1)], axis=0) for i in seqs]
        uy = [_mm_nt(x_scr[p, pk, :].astype(BF16), m.astype(BF16)) + u_scr[p, pk, :] for pk, m in zip(pick, s)]
        uv = [jnp.where(row8 < steps, m, gy_scr[p, pk, :]) for pk, m in zip(pick, uy)]
        for pk, m in zip(pick, uy):
            gy_scr[p, pk, :] = m
        upd = [_mm_tn(m.astype(BF16), bk_scr[p, pk, :].astype(BF16)) for pk, m in zip(pick, uv)]
        for i in seqs:
            s_new = s[i] * w_last[p][i:i + 1, :] + jnp.where(cm.same_head, upd[i], 0.0)
            s_out_ref[i, 2 * p] = s_new[:HEAD, :HEAD]
            s_out_ref[i, 2 * p + 1] = s_new[HEAD:, HEAD:]

    y = [gy_scr[p, CHUNK:, :] for p in range(n_pairs)]
    z = _wkv_post(y, r, kmod, v, tok(g_ref), tok(rk_ref), tok(gng_ref), tok(gnb_ref), cm)
    for l, zz in zip(lanes, z):
        z_ref[:, l] = zz


def _wkv_sample(tok, vres, chan, state):
    m, d = tok[0].shape
    nb = state.shape[0]
    has_vres = vres is not None
    n_pairs = WKV_SAMPLE_PAIRS
    width = n_pairs * PAIR
    tile = pl.BlockSpec((CHUNK, width), lambda i, p: (i, p))
    row = pl.BlockSpec((1, width), lambda i, p: (0, p))
    blk = (SEQ_PER_CHUNK, 2 * n_pairs, HEAD, HEAD)
    args = tuple(tok) + (tuple(vres) if has_vres else ()) + tuple(chan) + (state,)
    n_tok = len(tok) + (2 if has_vres else 0)
    return pl.pallas_call(
        functools.partial(_wkv_sample_kernel, has_vres, n_pairs),
        grid=(nb // SEQ_PER_CHUNK, d // width),
        in_specs=[tile] * n_tok + [row] * len(chan) + [pl.BlockSpec(blk, lambda i, p: (i, p, 0, 0))],
        out_specs=[tile, pl.BlockSpec(blk, lambda i, p: (i, p, 0, 0))],
        out_shape=[jax.ShapeDtypeStruct((m, d), BF16), jax.ShapeDtypeStruct(state.shape, F32)],
        scratch_shapes=[pltpu.VMEM((n_pairs, 2 * CHUNK, PAIR), F32)] * 4,
        compiler_params=_cparams("parallel", "parallel"),
        name="wkv_sample",
    )(*args)


def _out_ln_kernel(alpha, z_ref, x_ref, w_ref, g_ref, b_ref, o_ref, ob_ref):
    y = _layer_norm(alpha * x_ref[...] + _mm(z_ref[...], w_ref[...]), g_ref[...], b_ref[...])
    o_ref[...] = y
    ob_ref[...] = y.astype(BF16)


def _out_ln(alpha, z, x, w, layer, g, b):
    m, d = x.shape
    tile = pl.BlockSpec((TM, d), lambda i: (i, 0))
    return pl.pallas_call(
        functools.partial(_out_ln_kernel, alpha),
        grid=(m // TM,),
        in_specs=[tile, tile, pl.BlockSpec((None, d, d), lambda i: (layer, 0, 0)), _row(g), _row(b)],
        out_specs=[tile, tile],
        out_shape=[jax.ShapeDtypeStruct((m, d), F32), jax.ShapeDtypeStruct((m, d), BF16)],
        compiler_params=_cparams("parallel"),
        name="out_ln",
    )(z, x, w, g, b)


def _ffn_kernel(alpha, emit_weights, *refs):
    if emit_weights:
        (xb_ref, x_ref, wg_ref, wu_ref, wd_ref, g_ref, b_ref,
         o_ref, ob_ref, wgb_ref, wub_ref, wdb_ref, acc_ref) = refs
    else:
        xb_ref, x_ref, wg_ref, wu_ref, wd_ref, g_ref, b_ref, o_ref, ob_ref, acc_ref = refs
    f = pl.program_id(1)

    @pl.when(f == 0)
    def _():
        acc_ref[...] = jnp.zeros_like(acc_ref)

    if emit_weights:
        wgb_ref[...] = wg_ref[...].astype(BF16)
        wub_ref[...] = wu_ref[...].astype(BF16)
        wdb_ref[...] = wd_ref[...].astype(BF16)
        wg_ref, wu_ref, wd_ref = wgb_ref, wub_ref, wdb_ref
    xb = xb_ref[...]
    gate = _mm(xb, wg_ref[...])
    up = _mm(xb, wu_ref[...])
    act = (gate * _sigmoid(gate) * up).astype(BF16)
    for c in range(0, acc_ref.shape[1], FFN_DOWN_CHUNK):
        cols = slice(c, c + FFN_DOWN_CHUNK)
        acc_ref[:, cols] += _mm(act, wd_ref[:, cols])

    @pl.when(f == pl.num_programs(1) - 1)
    def _():
        y = _layer_norm(alpha * x_ref[...] + acc_ref[...], g_ref[...], b_ref[...])
        o_ref[...] = y
        ob_ref[...] = y.astype(BF16)


def _ffn(alpha, xb, x, wg, wu, wd, g, b):
    m, d = x.shape
    d_ff = wd.shape[0]
    tile = pl.BlockSpec((TM, d), lambda i, f: (i, 0))
    return pl.pallas_call(
        functools.partial(_ffn_kernel, alpha, False),
        grid=(m // TM, d_ff // TF),
        in_specs=[tile, tile,
                  pl.BlockSpec((d, TF), lambda i, f: (0, f)),
                  pl.BlockSpec((d, TF), lambda i, f: (0, f)),
                  pl.BlockSpec((TF, d), lambda i, f: (f, 0)),
                  _row(g), _row(b)],
        out_specs=[tile, tile],
        out_shape=[jax.ShapeDtypeStruct((m, d), F32), jax.ShapeDtypeStruct((m, d), BF16)],
        scratch_shapes=[pltpu.VMEM((TM, d), F32)],
        compiler_params=_cparams("parallel", "arbitrary"),
        name="ffn",
    )(xb, x, wg, wu, wd, g, b)


def _ffn_cast(alpha, xb, x, w_in, w_down, layer, g, b):
    m, d = x.shape
    assert m == TM
    d_ff = w_down.shape[1]
    nf = d_ff // TF_CAST
    tile = pl.BlockSpec((TM, d), lambda i, f: (i, 0))
    return pl.pallas_call(
        functools.partial(_ffn_kernel, alpha, True),
        grid=(1, nf),
        in_specs=[tile, tile,
                  pl.BlockSpec((None, d, TF_CAST), lambda i, f: (layer, 0, f)),
                  pl.BlockSpec((None, d, TF_CAST), lambda i, f: (layer, 0, f + nf)),
                  pl.BlockSpec((None, TF_CAST, d), lambda i, f: (layer, f, 0)),
                  _row(g), _row(b)],
        out_specs=[tile, tile,
                   pl.BlockSpec((d, TF_CAST), lambda i, f: (0, f)),
                   pl.BlockSpec((d, TF_CAST), lambda i, f: (0, f)),
                   pl.BlockSpec((TF_CAST, d), lambda i, f: (f, 0))],
        out_shape=[jax.ShapeDtypeStruct((m, d), F32), jax.ShapeDtypeStruct((m, d), BF16),
                   jax.ShapeDtypeStruct((d, d_ff), BF16), jax.ShapeDtypeStruct((d, d_ff), BF16),
                   jax.ShapeDtypeStruct((d_ff, d), BF16)],
        scratch_shapes=[pltpu.VMEM((TM, d), F32)],
        compiler_params=_cparams("arbitrary", "arbitrary"),
        name="ffn_cast",
    )(xb, x, w_in, w_in, w_down, g, b)


def _pool_prompt_kernel(alpha, tiles_per_seq, x_ref, halo_ref, w_ref, sc_ref, g_ref, b_ref, o_ref, ob_ref):
    tile_in_seq = pl.program_id(0) % tiles_per_seq
    x = x_ref[...]
    halo = jnp.where(tile_in_seq == 0, 0.0, halo_ref[...])
    gw = x.shape[1] // len(POOL_WINDOWS)
    pos = tile_in_seq * TM + lax.broadcasted_iota(jnp.int32, (TM, gw), 0)
    outs = []
    for gi, win in enumerate(POOL_WINDOWS):
        lanes = slice(gi * gw, (gi + 1) * gw)
        xg = x[:, lanes]
        s = jnp.concatenate([halo[:, lanes], xg], axis=0)
        span = 1
        while span < win:
            s = s[span:] + s[:-span]
            span *= 2
        first = HALO - (win - 1)
        cnt = jnp.minimum(win, pos + 1).astype(F32)
        dg = s[first:first + TM] / cnt - xg
        outs.append(_mm(dg.astype(BF16), w_ref[gi]))
    h = jnp.concatenate(outs, axis=1) * sc_ref[...]
    y = _layer_norm(alpha * x + h, g_ref[...], b_ref[...])
    o_ref[...] = y
    ob_ref[...] = y.astype(BF16)


def _pool_prompt(alpha, x, w, layer, scale, g, b, seq):
    m, d = x.shape
    tile = pl.BlockSpec((TM, d), lambda i: (i, 0))
    halo = pl.BlockSpec((HALO, d), lambda i: (jnp.maximum(i * (TM // HALO) - 1, 0), 0))
    return pl.pallas_call(
        functools.partial(_pool_prompt_kernel, alpha, seq // TM),
        grid=(m // TM,),
        in_specs=[tile, halo, pl.BlockSpec((None,) + w.shape[1:], lambda i: (layer, 0, 0, 0)),
                  _row(scale), _row(g), _row(b)],
        out_specs=[tile, tile],
        out_shape=[jax.ShapeDtypeStruct((m, d), F32), jax.ShapeDtypeStruct((m, d), BF16)],
        compiler_params=_cparams("parallel"),
        name="pool_prompt",
    )(x, x, w, scale, g, b)


def _pool_sample_kernel(ext_ref, w_ref, sc_ref, h_ref):
    steps = ext_ref.shape[0] - POOL_BUF
    gi = pl.program_id(0)
    ds = []
    for t in range(steps):
        cur = ext_ref[POOL_BUF + t]
        acc16 = cur
        sums = {}
        for i in range(1, max(POOL_WINDOWS)):
            acc16 = acc16 + ext_ref[POOL_BUF + t - i]
            if i + 1 in POOL_WINDOWS:
                sums[i + 1] = acc16
        d = sums[POOL_WINDOWS[-1]] * (1.0 / POOL_WINDOWS[-1])
        for j, win in enumerate(POOL_WINDOWS[:-1]):
            d = jnp.where(gi == j, sums[win] * (1.0 / win), d)
        ds.append((d - cur).astype(BF16))
    h_ref[...] = _mm(jnp.concatenate(ds, axis=0), w_ref[...]) * sc_ref[...]


def _pool_sample(ext, w, layer, scale):
    n, nb, d = ext.shape
    ng = len(POOL_WINDOWS)
    gw = d // ng
    return pl.pallas_call(
        _pool_sample_kernel,
        grid=(ng,),
        in_specs=[pl.BlockSpec((n, nb, gw), lambda gi: (0, 0, gi)),
                  pl.BlockSpec((None, None, gw, gw), lambda gi: (layer, gi, 0, 0)),
                  pl.BlockSpec((1, gw), lambda gi: (0, gi))],
        out_specs=pl.BlockSpec(((n - POOL_BUF) * nb, gw), lambda gi: (0, gi)),
        out_shape=jax.ShapeDtypeStruct(((n - POOL_BUF) * nb, d), F32),
        compiler_params=_cparams("parallel"),
        name="pool_sample",
    )(ext, w, scale)


def _add_ln_kernel(alpha, x_ref, h_ref, g_ref, b_ref, o_ref, ob_ref):
    y = _layer_norm(alpha * x_ref[...] + h_ref[...], g_ref[...], b_ref[...])
    o_ref[...] = y
    ob_ref[...] = y.astype(BF16)


def _add_ln(alpha, x, h, g, b):
    m, d = x.shape
    tile = pl.BlockSpec((TM, d), lambda i: (i, 0))
    return pl.pallas_call(
        functools.partial(_add_ln_kernel, alpha),
        grid=(m // TM,),
        in_specs=[tile, tile, _row(g), _row(b)],
        out_specs=[tile, tile],
        out_shape=[jax.ShapeDtypeStruct((m, d), F32), jax.ShapeDtypeStruct((m, d), BF16)],
        compiler_params=_cparams("parallel"),
        name="add_ln",
    )(x, h, g, b)


def _pad_lora(w_a, w_b):
    rank = w_a.shape[1]
    pad = (-rank) % LORA_PAD
    return (jnp.pad(w_a, ((0, 0), (0, pad))).astype(BF16), jnp.pad(w_b, ((0, pad), (0, 0))).astype(BF16))


def kernel(x_prompt, x_sample, state_wkv, state_shift, state_pool, ln_g, ln_b, rw_mu, rw_wr, rw_wk, rw_wv, rw_wo, rw_w0, rw_w1, rw_w2, rw_a0, rw_a1, rw_a2, rw_v0, rw_v1, rw_v2, rw_g1, rw_g2, rw_kk, rw_ka, rw_rk, rw_gn_g, rw_gn_b, pool_w, pool_scale, ffn_w_in, ffn_w_down):
    bp, seq, d = x_prompt.shape
    bs, steps, _ = x_sample.shape
    depth = ln_g.shape[0]
    n_mixers = 2
    alpha = float((2 * depth) ** 0.25)
    m_sample = bs * steps
    nblk = bs // SEQ_PER_CHUNK
    assert steps * SEQ_PER_CHUNK == CHUNK and bs % SEQ_PER_CHUNK == 0
    assert seq % WKV_ROWS == 0 and seq % TM == 0 and m_sample % TM == 0

    def sample_to_rows(a):
        return a.reshape(nblk, SEQ_PER_CHUNK, steps, d).transpose(0, 2, 1, 3).reshape(m_sample, d)

    def rows_to_sample(a):
        return a.reshape(nblk, steps, SEQ_PER_CHUNK, d).transpose(0, 2, 1, 3).reshape(bs, steps, d)

    wo, w_pool = rw_wo.astype(BF16), pool_w.astype(BF16)

    xp = x_prompt.reshape(bp * seq, d)
    xs = sample_to_rows(x_sample)
    xpb = xsb = None
    vf_p = vf_s = None
    new_wkv_p, new_wkv_s, new_shift_p, new_shift_s, new_pool_p, sample_pool_rows = [], [], [], [], [], []
    for i in range(depth):
        j = i // n_mixers
        lg, lb = ln_g[i, 0][None, :], ln_b[i, 0][None, :]
        xs4 = xs.reshape(nblk, steps, SEQ_PER_CHUNK, d)
        if i % n_mixers == 0:
            prev_s = jnp.concatenate([state_shift[j].reshape(nblk, 1, SEQ_PER_CHUNK, d), xs4[:, :-1]],
                                     axis=1).reshape(m_sample, d)
            new_shift_p.append(jnp.concatenate([xp[(b + 1) * seq - 1:(b + 1) * seq] for b in range(bp)]))
            new_shift_s.append(xs4[:, -1].reshape(bs, d))

            w1, w2 = _pad_lora(rw_w1[j], rw_w2[j])
            a1, a2 = _pad_lora(rw_a1[j], rw_a2[j])
            g1, g2 = rw_g1[j].astype(BF16), rw_g2[j].astype(BF16)
            if j == 0:
                params = (rw_mu[j], rw_w0[j][None, :], rw_a0[j][None, :], w1, w2, a1, a2, g1, g2)
            else:
                v1, v2 = _pad_lora(rw_v1[j - 1], rw_v2[j - 1])
                params = (rw_mu[j], rw_w0[j][None, :], rw_a0[j][None, :], rw_v0[j - 1][None, :],
                          w1, w2, a1, a2, v1, v2, g1, g2)
            chan = (rw_kk[j][None, :], rw_ka[j][None, :], rw_rk[j].reshape(1, d),
                    rw_gn_g[j][None, :], rw_gn_b[j][None, :])

            mixed = _rwkv_mix(xs, prev_s, seq, params)
            r, k, v, wr, wk, wv = _rkv_proj_cast(mixed[0], mixed[1], mixed[2], rw_wr, rw_wk, rw_wv, j)
            tok_s, vg_s = (r, k, v) + tuple(mixed[3:6]), (mixed[6] if j > 0 else None)
            mixed = _rwkv_mix(xp, None, seq, params)
            r, k, v = _rkv_proj(mixed[0], mixed[1], mixed[2], wr, wk, wv)
            tok_p, vg_p = (r, k, v) + tuple(mixed[3:6]), (mixed[6] if j > 0 else None)
            if j == 0:
                vf_p, vf_s = tok_p[2], tok_s[2]
            zp, s_p = _wkv_prompt(tok_p, None if j == 0 else (vf_p, vg_p), chan, bp, seq)
            zs, s_s = _wkv_sample(tok_s, None if j == 0 else (vf_s, vg_s), chan, state_wkv[j])
            new_wkv_p.append(s_p)
            new_wkv_s.append(s_s)
            xp, xpb = _out_ln(alpha, zp, xp, wo, j, lg, lb)
            xs, xsb = _out_ln(alpha, zs, xs, wo, j, lg, lb)
        else:
            xs_tb = xs4.transpose(1, 0, 2, 3).reshape(steps, bs, d)
            new_pool_p.append(jnp.stack([xp[(b + 1) * seq - POOL_BUF:(b + 1) * seq] for b in range(bp)]))
            sample_pool_rows.append(xs_tb.transpose(1, 0, 2))
            sc = pool_scale[j][None, :]
            ext = jnp.concatenate([state_pool[j].transpose(1, 0, 2), xs_tb], axis=0)
            h_tb = _pool_sample(ext, w_pool, j, sc)
            h_rows = h_tb.reshape(steps, nblk, SEQ_PER_CHUNK, d).transpose(1, 0, 2, 3).reshape(m_sample, d)
            xp, xpb = _pool_prompt(alpha, xp, w_pool, j, sc, lg, lb, seq)
            xs, xsb = _add_ln(alpha, xs, h_rows, lg, lb)
        lg, lb = ln_g[i, 1][None, :], ln_b[i, 1][None, :]
        xs, xsb, wg, wu, wd = _ffn_cast(alpha, xsb, xs, ffn_w_in, ffn_w_down, i, lg, lb)
        xp, xpb = _ffn(alpha, xpb, xp, wg, wu, wd, lg, lb)

    new_pool_s = jnp.concatenate([state_pool[:, :, steps:], jnp.stack(sample_pool_rows)], axis=2)
    return (xp.reshape(bp, seq, d), rows_to_sample(xs), jnp.stack(new_wkv_p), jnp.stack(new_shift_p),
            jnp.stack(new_pool_p), jnp.stack(new_wkv_s), jnp.stack(new_shift_s), new_pool_s)
```

```python
import functools
import math

import jax
import jax.numpy as jnp
from jax import lax
from jax.experimental import pallas as pl
from jax.experimental.pallas import tpu as pltpu

F32 = jnp.float32
BF16 = jnp.bfloat16

HEAD = 64
PAIR = 2 * HEAD
CHUNK = 64
SEQ_PER_CHUNK = 16
GN_EPS = 64e-5
LN_EPS = 1e-5
POOL_WINDOWS = (2, 4, 8, 16)
POOL_BUF = 15
HALO = 16
SHIFT_HALO = 8
LORA_PAD = 128

TM = 512
TM_MIX = 256
TM_FFN = 1024
TM_RKV = 1024
TN = 512
TN_CAST = 256
TF = 512
TF_CAST = 256
FFN_DOWN_CHUNK = 512
FFN_HIDDEN_CHUNK = 512
WKV_ROWS = 256
WKV_PAIRS = 16
WKV_SAMPLE_PAIRS = 4
VMEM_LIMIT = 56 * 1024 * 1024


def _cparams(*sem):
    return pltpu.CompilerParams(dimension_semantics=sem, vmem_limit_bytes=VMEM_LIMIT)


def _mm(a, b):
    return jnp.dot(a, b, preferred_element_type=F32)


def _mm_nt(a, b):
    return lax.dot_general(a, b, (((1,), (1,)), ((), ())), preferred_element_type=F32)


def _mm_tn(a, b):
    return lax.dot_general(a, b, (((0,), (0,)), ((), ())), preferred_element_type=F32)


def _sigmoid(x):
    return 1.0 / (1.0 + jnp.exp(-x))


def _layer_norm(v, g, b):
    mu = jnp.mean(v, axis=-1, keepdims=True)
    c = v - mu
    var = jnp.mean(c * c, axis=-1, keepdims=True)
    return c * lax.rsqrt(var + LN_EPS) * g + b


def _split2(x):
    hi = x.astype(BF16)
    return hi, (x - hi.astype(F32)).astype(BF16)


def _row(a):
    return pl.BlockSpec((1, a.shape[-1]), lambda *_: (0, 0))


def _mix_kernel(has_vres, tiles_per_seq, *refs):
    if has_vres:
        (x_ref, prev_ref, mu_ref, w0_ref, a0_ref, v0_ref, w1_ref, w2_ref, a1_ref, a2_ref,
         v1_ref, v2_ref, g1_ref, g2_ref, xr_ref, xk_ref, xv_ref, lw_ref, ag_ref, g_ref, vg_ref) = refs
    else:
        (x_ref, prev_ref, mu_ref, w0_ref, a0_ref, w1_ref, w2_ref, a1_ref, a2_ref,
         g1_ref, g2_ref, xr_ref, xk_ref, xv_ref, lw_ref, ag_ref, g_ref) = refs
    x = x_ref[...]
    if tiles_per_seq is None:
        xprev = prev_ref[...]
    else:
        before = jnp.where(pl.program_id(0) % tiles_per_seq == 0, 0.0, prev_ref[SHIFT_HALO - 1:SHIFT_HALO, :])
        first_row = lax.broadcasted_iota(jnp.int32, x.shape, 0) == 0
        xprev = jnp.where(first_row, before, pltpu.roll(x, 1, 0))
    xx = xprev - x

    def mix(j):
        return (x + xx * mu_ref[j:j + 1, :]).astype(BF16)

    xr_ref[...] = mix(0)
    xk_ref[...] = mix(2)
    xv = mix(3)
    xv_ref[...] = xv

    u = w0_ref[...] + _mm(jnp.tanh(_mm(mix(1), w1_ref[...])).astype(BF16), w2_ref[...])
    z = -u
    softplus = jnp.maximum(z, 0.0) + jnp.log(1.0 + jnp.exp(-jnp.abs(z)))
    lw_ref[...] = -jnp.exp(-softplus - 0.5)
    ag_ref[...] = _sigmoid(a0_ref[...] + _mm(_mm(mix(4), a1_ref[...]).astype(BF16), a2_ref[...]))
    g_ref[...] = _mm(_sigmoid(_mm(mix(5), g1_ref[...])).astype(BF16), g2_ref[...])
    if has_vres:
        vg_ref[...] = _sigmoid(v0_ref[...] + _mm(_mm(xv, v1_ref[...]).astype(BF16), v2_ref[...]))


def _rwkv_mix(x, xprev, seq, params):
    m, d = x.shape
    has_vres = len(params) == 12
    tile = pl.BlockSpec((TM_MIX, d), lambda i: (i, 0))
    if xprev is None:
        prev, prev_spec = x, pl.BlockSpec(
            (SHIFT_HALO, d), lambda i: (jnp.maximum(i * (TM_MIX // SHIFT_HALO) - 1, 0), 0))
        tiles_per_seq = seq // TM_MIX
    else:
        prev, prev_spec, tiles_per_seq = xprev, tile, None

    def full(a):
        return pl.BlockSpec(a.shape, lambda i: (0,) * a.ndim)

    n_f32_out = 4 if has_vres else 3
    out_shape = [jax.ShapeDtypeStruct((m, d), BF16)] * 3 + [jax.ShapeDtypeStruct((m, d), F32)] * n_f32_out
    return pl.pallas_call(
        functools.partial(_mix_kernel, has_vres, tiles_per_seq),
        grid=(m // TM_MIX,),
        in_specs=[tile, prev_spec] + [full(a) for a in params],
        out_specs=[tile] * len(out_shape),
        out_shape=out_shape,
        compiler_params=_cparams("parallel"),
        name="rwkv_mix",
    )(x, prev, *params)


def _rkv_kernel(xr_ref, xk_ref, xv_ref, wr_ref, wk_ref, wv_ref, r_ref, k_ref, v_ref):
    r_ref[...] = _mm(xr_ref[...], wr_ref[...])
    k_ref[...] = _mm(xk_ref[...], wk_ref[...])
    v_ref[...] = _mm(xv_ref[...], wv_ref[...])


def _rkv_proj(xr, xk, xv, wr, wk, wv):
    m, d = xr.shape
    xs = pl.BlockSpec((TM_RKV, d), lambda n, i: (i, 0))
    ws = pl.BlockSpec((d, TN), lambda n, i: (0, n))
    os = pl.BlockSpec((TM_RKV, TN), lambda n, i: (i, n))
    return pl.pallas_call(
        _rkv_kernel,
        grid=(d // TN, m // TM_RKV),
        in_specs=[xs, xs, xs, ws, ws, ws],
        out_specs=[os, os, os],
        out_shape=[jax.ShapeDtypeStruct((m, d), F32)] * 3,
        compiler_params=_cparams("parallel", "arbitrary"),
        name="rkv_proj",
    )(xr, xk, xv, wr, wk, wv)


def _rkv_cast_kernel(xr_ref, xk_ref, xv_ref, wr_ref, wk_ref, wv_ref, r_ref, k_ref, v_ref, wrb_ref, wkb_ref, wvb_ref):
    for x_ref, w_ref, o_ref, wb_ref in ((xr_ref, wr_ref, r_ref, wrb_ref), (xk_ref, wk_ref, k_ref, wkb_ref),
                                        (xv_ref, wv_ref, v_ref, wvb_ref)):
        wb = w_ref[...].astype(BF16)
        wb_ref[...] = wb
        o_ref[...] = _mm(x_ref[...], wb)


def _rkv_proj_cast(xr, xk, xv, wr, wk, wv, layer):
    m, d = xr.shape
    assert m == TM
    xs = pl.BlockSpec((TM, d), lambda n: (0, 0))
    ws = pl.BlockSpec((None, d, TN_CAST), lambda n: (layer, 0, n))
    os = pl.BlockSpec((TM, TN_CAST), lambda n: (0, n))
    wbs = pl.BlockSpec((d, TN_CAST), lambda n: (0, n))
    return pl.pallas_call(
        _rkv_cast_kernel,
        grid=(d // TN_CAST,),
        in_specs=[xs, xs, xs, ws, ws, ws],
        out_specs=[os, os, os, wbs, wbs, wbs],
        out_shape=[jax.ShapeDtypeStruct((m, d), F32)] * 3 + [jax.ShapeDtypeStruct((d, d), BF16)] * 3,
        compiler_params=_cparams("parallel"),
        name="rkv_proj_cast",
    )(xr, xk, xv, wr, wk, wv)


class _ChunkMasks:
    def __init__(self, sample):
        row = lax.broadcasted_iota(jnp.int32, (CHUNK, PAIR), 0)
        lane = lax.broadcasted_iota(jnp.int32, (CHUNK, PAIR), 1)
        col = lane & (CHUNK - 1)
        r2 = lax.broadcasted_iota(jnp.int32, (CHUNK, CHUNK), 0)
        c2 = lax.broadcasted_iota(jnp.int32, (CHUNK, CHUNK), 1)
        if sample:
            shift = int(math.log2(SEQ_PER_CHUNK))
            seq_mask = SEQ_PER_CHUNK - 1
            same = (row & seq_mask) == (col & seq_mask)
            self.strict = same & ((col >> shift) < (row >> shift))
            self.incl = same & ((col >> shift) <= (row >> shift))
            tri = ((r2 & seq_mask) == (c2 & seq_mask)) & ((c2 >> shift) <= (r2 >> shift))
            self.squarings = 1
        else:
            self.strict = col < row
            self.incl = col <= row
            tri = c2 <= r2
            self.squarings = 5
        self.sample = sample
        self.eye = jnp.where(row == col, 1.0, 0.0).astype(F32)
        self.tri = jnp.where(tri, 1.0, 0.0).astype(BF16)
        self.head0 = lane < HEAD
        r128 = lax.broadcasted_iota(jnp.int32, (PAIR, PAIR), 0)
        c128 = lax.broadcasted_iota(jnp.int32, (PAIR, PAIR), 1)
        self.same_head = (r128 < HEAD) == (c128 < HEAD)
        self.diag128 = r128 == c128


def _blk(y, cm):
    return jnp.concatenate([jnp.where(cm.head0, y, 0.0), jnp.where(cm.head0, 0.0, y)], axis=0).astype(BF16)


def _headsum(xs, cm):
    parts = [(jnp.sum(jnp.where(cm.head0, x, 0.0), axis=1, keepdims=True),
              jnp.sum(jnp.where(cm.head0, 0.0, x), axis=1, keepdims=True)) for x in xs]
    return [jnp.where(cm.head0, s0, s1) for s0, s1 in parts]


def _wkv_pre(kraw, v, ag, vf, vg, kkw, ka, cm):
    kkv = [k * w for k, w in zip(kraw, kkw)]
    sumsq = _headsum([x * x for x in kkv], cm)
    kkn = [x / jnp.maximum(jnp.sqrt(s), 1e-12) for x, s in zip(kkv, sumsq)]
    kmod = [k * (1.0 + (g - 1.0) * c) for k, g, c in zip(kraw, ag, ka)]
    if vf is not None:
        v = [x + (f - x) * g for x, f, g in zip(v, vf, vg)]
    return kmod, v, [-x for x in kkn], [x * g for x, g in zip(kkn, ag)]


def _wkv_post(y, r, kmod, v, g, rk, gng, gnb, cm):
    inv_n = 1.0 / HEAD
    n = len(y)
    sums = _headsum(list(y) + [a * b * c for a, b, c in zip(r, kmod, rk)], cm)
    yc = [a - s * inv_n for a, s in zip(y, sums[:n])]
    var = _headsum([c * c for c in yc], cm)
    return [((c * lax.rsqrt(s * inv_n + GN_EPS) * gg + gb + bs * vv) * gt).astype(BF16)
            for c, s, gg, gb, bs, vv, gt in zip(yc, var, gng, gnb, sums[n:], v, g)]


def _chunk_prep(r, lw, k, v, a, b, cm):
    splits = [_split2(x) for x in lw]
    cum = [_mm(cm.tri, hi) + _mm(cm.tri, lo) for hi, lo in splits]
    if cm.sample:
        cl = [jnp.concatenate([c[CHUNK - SEQ_PER_CHUNK:, :]] * (CHUNK // SEQ_PER_CHUNK), axis=0) for c in cum]
    else:
        cl = [jnp.broadcast_to(c[CHUNK - 1:CHUNK, :], c.shape) for c in cum]
    w_inv = [jnp.exp(-c) for c in cum]
    w_last = [jnp.exp(c) for c in cl]
    rt = [x * jnp.exp(c) for x, c in zip(r, cum)]
    at = [x * jnp.exp(c - l) for x, c, l in zip(a, cum, lw)]
    bt = [x * w for x, w in zip(b, w_inv)]
    kt = [x * w for x, w in zip(k, w_inv)]
    x = [jnp.concatenate([p, q], axis=0).astype(BF16) for p, q in zip(at, rt)]
    blk_bk = [jnp.concatenate([_blk(p, cm), _blk(q, cm)], axis=0) for p, q in zip(bt, kt)]
    prod = [_mm_nt(p, q) for p, q in zip(x, blk_bk)]
    nak = [jnp.where(cm.strict, y[:CHUNK, PAIR:], 0.0).astype(BF16) for y in prod]
    mrbk = [jnp.concatenate([jnp.where(cm.incl, y[CHUNK:, :PAIR], 0.0),
                             jnp.where(cm.incl, y[CHUNK:, PAIR:], 0.0)], axis=1).astype(BF16) for y in prod]
    p = [jnp.where(cm.strict, y[:CHUNK, :PAIR], 0.0) for y in prod]
    t = [cm.eye + y for y in p]
    p = [_mm(y.astype(BF16), _blk(y, cm)) for y in p]
    for _ in range(cm.squarings - 1):
        both = [_mm(jnp.concatenate([y, z], axis=0).astype(BF16), _blk(y, cm)) for y, z in zip(p, t)]
        p = [y[:CHUNK] for y in both]
        t = [z + y[CHUNK:] for z, y in zip(t, both)]
    t = [z + _mm(z.astype(BF16), _blk(y, cm)) for z, y in zip(t, p)]
    tb = [y.astype(BF16) for y in t]
    blk_v = [_blk(y, cm) for y in v]
    q = [_mm(m, w) for m, w in zip(nak, blk_v)]
    au = [_mm(m, jnp.concatenate([_blk(y, cm), _blk(z, cm)], axis=1)) for m, y, z in zip(tb, at, q)]
    ah = [y[:, :PAIR] for y in au]
    uh = [y[:, PAIR:] for y in au]
    rh = [y + _mm(m[:, :PAIR], _blk(z, cm)) for y, m, z in zip(rt, mrbk, ah)]
    yh = [_mm(m, jnp.concatenate([_blk(z, cm), w], axis=0)) for m, z, w in zip(mrbk, uh, blk_v)]
    bh = [y * w for y, w in zip(bt, w_last)]
    kh = [y * w for y, w in zip(kt, w_last)]
    return ah, uh, rh, yh, bh, kh, w_last


def _wkv_prompt_kernel(has_vres, n_chunks, n_pairs, *refs):
    if has_vres:
        (r_ref, k_ref, v_ref, lw_ref, ag_ref, g_ref, vf_ref, vg_ref,
         kkw_ref, ka_ref, rk_ref, gng_ref, gnb_ref, z_ref, s_ref, s_scr) = refs
    else:
        (r_ref, k_ref, v_ref, lw_ref, ag_ref, g_ref,
         kkw_ref, ka_ref, rk_ref, gng_ref, gnb_ref, z_ref, s_ref, s_scr) = refs
    step = pl.program_id(1)

    @pl.when(step == 0)
    def _():
        s_scr[...] = jnp.zeros_like(s_scr)

    cm = _ChunkMasks(sample=False)
    lanes = [slice(p * PAIR, (p + 1) * PAIR) for p in range(n_pairs)]

    def chunk(c, carry):
        rows = pl.ds(pl.multiple_of(c * CHUNK, CHUNK), CHUNK)

        def tok(ref):
            return [ref[rows, l] for l in lanes]

        def chan(ref):
            return [ref[:, l] for l in lanes]

        r = tok(r_ref)
        kmod, v, a, b = _wkv_pre(tok(k_ref), tok(v_ref), tok(ag_ref),
                                 tok(vf_ref) if has_vres else None, tok(vg_ref) if has_vres else None,
                                 chan(kkw_ref), chan(ka_ref), cm)
        ah, uh, rh, yh, bh, kh, w_last = _chunk_prep(r, tok(lw_ref), kmod, v, a, b, cm)
        s16 = [s_scr[p].astype(BF16) for p in range(n_pairs)]
        y = [_mm_nt(x.astype(BF16), h) + z for x, h, z in zip(rh, s16, yh)]
        bh16 = [x.astype(BF16) for x in bh]
        ab = [_mm_tn(x.astype(BF16), w) for x, w in zip(ah, bh16)]
        uvbk = [_mm_tn(jnp.concatenate([p, q], axis=0).astype(BF16),
                       jnp.concatenate([w, x.astype(BF16)], axis=0))
                for p, q, w, x in zip(uh, v, bh16, kh)]
        phi = [(jnp.where(cm.diag128, jnp.broadcast_to(w[:1, :], (PAIR, PAIR)), 0.0)
                + jnp.where(cm.same_head, m, 0.0)).astype(BF16) for w, m in zip(w_last, ab)]
        s_new = [_mm(h, f) + jnp.where(cm.same_head, m, 0.0) for h, f, m in zip(s16, phi, uvbk)]
        for p in range(n_pairs):
            s_scr[p] = s_new[p]
        z = _wkv_post(y, r, kmod, v, tok(g_ref), chan(rk_ref), chan(gng_ref), chan(gnb_ref), cm)
        for l, zz in zip(lanes, z):
            z_ref[rows, l] = zz
        return carry

    lax.fori_loop(0, n_chunks, chunk, 0)

    @pl.when(step == pl.num_programs(1) - 1)
    def _():
        for p in range(n_pairs):
            s = s_scr[p]
            s_ref[0, 2 * p] = s[:HEAD, :HEAD]
            s_ref[0, 2 * p + 1] = s[HEAD:, HEAD:]


def _wkv_prompt(tok, vres, chan, batch, seq):
    m, d = tok[0].shape
    has_vres = vres is not None
    assert d == WKV_PAIRS * PAIR
    steps = seq // WKV_ROWS
    tile = pl.BlockSpec((WKV_ROWS, d), lambda b, s: (b * steps + s, 0))
    args = tuple(tok) + (tuple(vres) if has_vres else ()) + tuple(chan)
    n_tok = len(tok) + (2 if has_vres else 0)
    n_heads = d // HEAD
    return pl.pallas_call(
        functools.partial(_wkv_prompt_kernel, has_vres, WKV_ROWS // CHUNK, WKV_PAIRS),
        grid=(batch, steps),
        in_specs=[tile] * n_tok + [_row(c) for c in chan],
        out_specs=[tile, pl.BlockSpec((1, n_heads, HEAD, HEAD), lambda b, s: (b, 0, 0, 0))],
        out_shape=[jax.ShapeDtypeStruct((m, d), BF16),
                   jax.ShapeDtypeStruct((batch, n_heads, HEAD, HEAD), F32)],
        scratch_shapes=[pltpu.VMEM((WKV_PAIRS, PAIR, PAIR), F32)],
        compiler_params=_cparams("parallel", "arbitrary"),
        name="wkv_prompt",
    )(*args)


def _wkv_sample_kernel(has_vres, n_pairs, *refs):
    if has_vres:
        (r_ref, k_ref, v_ref, lw_ref, ag_ref, g_ref, vf_ref, vg_ref,
         kkw_ref, ka_ref, rk_ref, gng_ref, gnb_ref, s_in_ref, z_ref, s_out_ref,
         x_scr, u_scr, bk_scr, gy_scr) = refs
    else:
        (r_ref, k_ref, v_ref, lw_ref, ag_ref, g_ref,
         kkw_ref, ka_ref, rk_ref, gng_ref, gnb_ref, s_in_ref, z_ref, s_out_ref,
         x_scr, u_scr, bk_scr, gy_scr) = refs
    cm = _ChunkMasks(sample=True)
    row8 = lax.broadcasted_iota(jnp.int32, (8, PAIR), 0)
    zeros_h = jnp.zeros((HEAD, HEAD), F32)
    steps = CHUNK // SEQ_PER_CHUNK
    lanes = [slice(p * PAIR, (p + 1) * PAIR) for p in range(n_pairs)]

    def tok(ref):
        return [ref[:, l] for l in lanes]

    r = tok(r_ref)
    kmod, v, a, b = _wkv_pre(tok(k_ref), tok(v_ref), tok(ag_ref),
                             tok(vf_ref) if has_vres else None, tok(vg_ref) if has_vres else None,
                             tok(kkw_ref), tok(ka_ref), cm)
    ah, uh, rh, yh, bh, kh, w_last = _chunk_prep(r, tok(lw_ref), kmod, v, a, b, cm)
    for p in range(n_pairs):
        x_scr[p, 0:CHUNK, :] = ah[p]
        x_scr[p, CHUNK:, :] = rh[p]
        u_scr[p, 0:CHUNK, :] = uh[p]
        u_scr[p, CHUNK:, :] = yh[p]
        bk_scr[p, 0:CHUNK, :] = bh[p]
        bk_scr[p, CHUNK:, :] = kh[p]
        gy_scr[p, CHUNK:, :] = v[p]

    seqs = range(SEQ_PER_CHUNK)
    for p in range(n_pairs):
        pick = [pl.ds(i, 2 * steps, stride=SEQ_PER_CHUNK) for i in seqs]
        s = [jnp.concatenate([jnp.concatenate([s_in_ref[i, 2 * p], zeros_h], axis=1),
                              jnp.concatenate([zeros_h, s_in_ref[i, 2 * p + 1]], axis=1)], axis=0) for i in seqs]
        uy = [_mm_nt(x_scr[p, pk, :].astype(BF16), m.astype(BF16)) + u_scr[p, pk, :] for pk, m in zip(pick, s)]
        uv = [jnp.where(row8 < steps, m, gy_scr[p, pk, :]) for pk, m in zip(pick, uy)]
        for pk, m in zip(pick, uy):
            gy_scr[p, pk, :] = m
        upd = [_mm_tn(m.astype(BF16), bk_scr[p, pk, :].astype(BF16)) for pk, m in zip(pick, uv)]
        for i in seqs:
            s_new = s[i] * w_last[p][i:i + 1, :] + jnp.where(cm.same_head, upd[i], 0.0)
            s_out_ref[i, 2 * p] = s_new[:HEAD, :HEAD]
            s_out_ref[i, 2 * p + 1] = s_new[HEAD:, HEAD:]

    y = [gy_scr[p, CHUNK:, :] for p in range(n_pairs)]
    z = _wkv_post(y, r, kmod, v, tok(g_ref), tok(rk_ref), tok(gng_ref), tok(gnb_ref), cm)
    for l, zz in zip(lanes, z):
        z_ref[:, l] = zz


def _wkv_sample(tok, vres, chan, state):
    m, d = tok[0].shape
    nb = state.shape[0]
    has_vres = vres is not None
    n_pairs = WKV_SAMPLE_PAIRS
    width = n_pairs * PAIR
    tile = pl.BlockSpec((CHUNK, width), lambda i, p: (i, p))
    row = pl.BlockSpec((1, width), lambda i, p: (0, p))
    blk = (SEQ_PER_CHUNK, 2 * n_pairs, HEAD, HEAD)
    args = tuple(tok) + (tuple(vres) if has_vres else ()) + tuple(chan) + (state,)
    n_tok = len(tok) + (2 if has_vres else 0)
    return pl.pallas_call(
        functools.partial(_wkv_sample_kernel, has_vres, n_pairs),
        grid=(nb // SEQ_PER_CHUNK, d // width),
        in_specs=[tile] * n_tok + [row] * len(chan) + [pl.BlockSpec(blk, lambda i, p: (i, p, 0, 0))],
        out_specs=[tile, pl.BlockSpec(blk, lambda i, p: (i, p, 0, 0))],
        out_shape=[jax.ShapeDtypeStruct((m, d), BF16), jax.ShapeDtypeStruct(state.shape, F32)],
        scratch_shapes=[pltpu.VMEM((n_pairs, 2 * CHUNK, PAIR), F32)] * 4,
        compiler_params=_cparams("parallel", "parallel"),
        name="wkv_sample",
    )(*args)


def _out_ln_kernel(alpha, z_ref, x_ref, w_ref, g_ref, b_ref, o_ref, ob_ref):
    y = _layer_norm(alpha * x_ref[...] + _mm(z_ref[...], w_ref[...]), g_ref[...], b_ref[...])
    o_ref[...] = y
    ob_ref[...] = y.astype(BF16)


def _out_ln(alpha, z, x, w, layer, g, b):
    m, d = x.shape
    tile = pl.BlockSpec((TM, d), lambda i: (i, 0))
    return pl.pallas_call(
        functools.partial(_out_ln_kernel, alpha),
        grid=(m // TM,),
        in_specs=[tile, tile, pl.BlockSpec((None, d, d), lambda i: (layer, 0, 0)), _row(g), _row(b)],
        out_specs=[tile, tile],
        out_shape=[jax.ShapeDtypeStruct((m, d), F32), jax.ShapeDtypeStruct((m, d), BF16)],
        compiler_params=_cparams("parallel"),
        name="out_ln",
    )(z, x, w, g, b)


def _ffn_kernel(alpha, emit_weights, *refs):
    if emit_weights:
        (xb_ref, x_hbm, wg_ref, wu_ref, wd_ref, g_ref, b_ref,
         o_ref, wgb_ref, wub_ref, wdb_ref, xres_ref, sem) = refs
    else:
        xb_ref, x_hbm, wg_ref, wu_ref, wd_ref, g_ref, b_ref, o_ref, xres_ref, sem = refs
    f = pl.program_id(1)
    rows = o_ref.shape[0]

    def residual_copy():
        start = pl.multiple_of(pl.program_id(0) * rows, rows)
        return pltpu.make_async_copy(x_hbm.at[pl.ds(start, rows), :], xres_ref, sem)

    @pl.when(f == 0)
    def _():
        residual_copy().start()
        o_ref[...] = jnp.zeros_like(o_ref)

    if emit_weights:
        wgb_ref[...] = wg_ref[...].astype(BF16)
        wub_ref[...] = wu_ref[...].astype(BF16)
        wdb_ref[...] = wd_ref[...].astype(BF16)
        wg_ref, wu_ref, wd_ref = wgb_ref, wub_ref, wdb_ref
    xb = xb_ref[...]
    for h in range(0, wg_ref.shape[1], FFN_HIDDEN_CHUNK):
        hid = slice(h, h + FFN_HIDDEN_CHUNK)
        gate = _mm(xb, wg_ref[:, hid])
        up = _mm(xb, wu_ref[:, hid])
        act = (gate * _sigmoid(gate) * up).astype(BF16)
        for c in range(0, o_ref.shape[1], FFN_DOWN_CHUNK):
            cols = slice(c, c + FFN_DOWN_CHUNK)
            o_ref[:, cols] += _mm(act, wd_ref[hid, cols])

    @pl.when(f == pl.num_programs(1) - 1)
    def _():
        residual_copy().wait()
        o_ref[...] = _layer_norm(alpha * xres_ref[...] + o_ref[...], g_ref[...], b_ref[...])


def _ffn(alpha, xb, x, wg, wu, wd, g, b):
    m, d = x.shape
    d_ff = wd.shape[0]
    tile = pl.BlockSpec((TM_FFN, d), lambda i, f: (i, 0))
    in_tile = pl.BlockSpec((TM_FFN, d), lambda i, f: (i, 0), pipeline_mode=pl.Buffered(1))
    return pl.pallas_call(
        functools.partial(_ffn_kernel, alpha, False),
        grid=(m // TM_FFN, d_ff // TF),
        in_specs=[in_tile, pl.BlockSpec(memory_space=pl.ANY),
                  pl.BlockSpec((d, TF), lambda i, f: (0, f)),
                  pl.BlockSpec((d, TF), lambda i, f: (0, f)),
                  pl.BlockSpec((TF, d), lambda i, f: (f, 0)),
                  _row(g), _row(b)],
        out_specs=tile,
        out_shape=jax.ShapeDtypeStruct((m, d), F32),
        scratch_shapes=[pltpu.VMEM((TM_FFN, d), F32), pltpu.SemaphoreType.DMA(())],
        compiler_params=_cparams("parallel", "arbitrary"),
        name="ffn",
    )(xb, x, wg, wu, wd, g, b)


def _ffn_cast(alpha, xb, x, w_in, w_down, layer, g, b):
    m, d = x.shape
    assert m == TM
    d_ff = w_down.shape[1]
    nf = d_ff // TF_CAST
    tile = pl.BlockSpec((TM, d), lambda i, f: (i, 0))
    return pl.pallas_call(
        functools.partial(_ffn_kernel, alpha, True),
        grid=(1, nf),
        in_specs=[tile, pl.BlockSpec(memory_space=pl.ANY),
                  pl.BlockSpec((None, d, TF_CAST), lambda i, f: (layer, 0, f)),
                  pl.BlockSpec((None, d, TF_CAST), lambda i, f: (layer, 0, f + nf)),
                  pl.BlockSpec((None, TF_CAST, d), lambda i, f: (layer, f, 0)),
                  _row(g), _row(b)],
        out_specs=[tile,
                   pl.BlockSpec((d, TF_CAST), lambda i, f: (0, f)),
                   pl.BlockSpec((d, TF_CAST), lambda i, f: (0, f)),
                   pl.BlockSpec((TF_CAST, d), lambda i, f: (f, 0))],
        out_shape=[jax.ShapeDtypeStruct((m, d), F32),
                   jax.ShapeDtypeStruct((d, d_ff), BF16), jax.ShapeDtypeStruct((d, d_ff), BF16),
                   jax.ShapeDtypeStruct((d_ff, d), BF16)],
        scratch_shapes=[pltpu.VMEM((TM, d), F32), pltpu.SemaphoreType.DMA(())],
        compiler_params=_cparams("arbitrary", "arbitrary"),
        name="ffn_cast",
    )(xb, x, w_in, w_in, w_down, g, b)


def _pool_prompt_kernel(alpha, tiles_per_seq, x_ref, halo_ref, w_ref, sc_ref, g_ref, b_ref, o_ref, ob_ref):
    tile_in_seq = pl.program_id(0) % tiles_per_seq
    x = x_ref[...]
    halo = jnp.where(tile_in_seq == 0, 0.0, halo_ref[...])
    gw = x.shape[1] // len(POOL_WINDOWS)
    pos = tile_in_seq * TM + lax.broadcasted_iota(jnp.int32, (TM, gw), 0)
    outs = []
    for gi, win in enumerate(POOL_WINDOWS):
        lanes = slice(gi * gw, (gi + 1) * gw)
        xg = x[:, lanes]
        s = jnp.concatenate([halo[:, lanes], xg], axis=0)
        span = 1
        while span < win:
            s = s[span:] + s[:-span]
            span *= 2
        first = HALO - (win - 1)
        cnt = jnp.minimum(win, pos + 1).astype(F32)
        dg = s[first:first + TM] / cnt - xg
        outs.append(_mm(dg.astype(BF16), w_ref[gi]))
    h = jnp.concatenate(outs, axis=1) * sc_ref[...]
    y = _layer_norm(alpha * x + h, g_ref[...], b_ref[...])
    o_ref[...] = y
    ob_ref[...] = y.astype(BF16)


def _pool_prompt(alpha, x, w, layer, scale, g, b, seq):
    m, d = x.shape
    tile = pl.BlockSpec((TM, d), lambda i: (i, 0))
    halo = pl.BlockSpec((HALO, d), lambda i: (jnp.maximum(i * (TM // HALO) - 1, 0), 0))
    return pl.pallas_call(
        functools.partial(_pool_prompt_kernel, alpha, seq // TM),
        grid=(m // TM,),
        in_specs=[tile, halo, pl.BlockSpec((None,) + w.shape[1:], lambda i: (layer, 0, 0, 0)),
                  _row(scale), _row(g), _row(b)],
        out_specs=[tile, tile],
        out_shape=[jax.ShapeDtypeStruct((m, d), F32), jax.ShapeDtypeStruct((m, d), BF16)],
        compiler_params=_cparams("parallel"),
        name="pool_prompt",
    )(x, x, w, scale, g, b)


def _pool_sample_kernel(ext_ref, w_ref, sc_ref, h_ref):
    steps = ext_ref.shape[0] - POOL_BUF
    gi = pl.program_id(0)
    ds = []
    for t in range(steps):
        cur = ext_ref[POOL_BUF + t]
        acc16 = cur
        sums = {}
        for i in range(1, max(POOL_WINDOWS)):
            acc16 = acc16 + ext_ref[POOL_BUF + t - i]
            if i + 1 in POOL_WINDOWS:
                sums[i + 1] = acc16
        d = sums[POOL_WINDOWS[-1]] * (1.0 / POOL_WINDOWS[-1])
        for j, win in enumerate(POOL_WINDOWS[:-1]):
            d = jnp.where(gi == j, sums[win] * (1.0 / win), d)
        ds.append((d - cur).astype(BF16))
    h_ref[...] = _mm(jnp.concatenate(ds, axis=0), w_ref[...]) * sc_ref[...]


def _pool_sample(ext, w, layer, scale):
    n, nb, d = ext.shape
    ng = len(POOL_WINDOWS)
    gw = d // ng
    return pl.pallas_call(
        _pool_sample_kernel,
        grid=(ng,),
        in_specs=[pl.BlockSpec((n, nb, gw), lambda gi: (0, 0, gi)),
                  pl.BlockSpec((None, None, gw, gw), lambda gi: (layer, gi, 0, 0)),
                  pl.BlockSpec((1, gw), lambda gi: (0, gi))],
        out_specs=pl.BlockSpec(((n - POOL_BUF) * nb, gw), lambda gi: (0, gi)),
        out_shape=jax.ShapeDtypeStruct(((n - POOL_BUF) * nb, d), F32),
        compiler_params=_cparams("parallel"),
        name="pool_sample",
    )(ext, w, scale)


def _add_ln_kernel(alpha, x_ref, h_ref, g_ref, b_ref, o_ref, ob_ref):
    y = _layer_norm(alpha * x_ref[...] + h_ref[...], g_ref[...], b_ref[...])
    o_ref[...] = y
    ob_ref[...] = y.astype(BF16)


def _add_ln(alpha, x, h, g, b):
    m, d = x.shape
    tile = pl.BlockSpec((TM, d), lambda i: (i, 0))
    return pl.pallas_call(
        functools.partial(_add_ln_kernel, alpha),
        grid=(m // TM,),
        in_specs=[tile, tile, _row(g), _row(b)],
        out_specs=[tile, tile],
        out_shape=[jax.ShapeDtypeStruct((m, d), F32), jax.ShapeDtypeStruct((m, d), BF16)],
        compiler_params=_cparams("parallel"),
        name="add_ln",
    )(x, h, g, b)


def _pad_lora(w_a, w_b):
    rank = w_a.shape[1]
    pad = (-rank) % LORA_PAD
    return (jnp.pad(w_a, ((0, 0), (0, pad))).astype(BF16), jnp.pad(w_b, ((0, pad), (0, 0))).astype(BF16))


def kernel(x_prompt, x_sample, state_wkv, state_shift, state_pool, ln_g, ln_b, rw_mu, rw_wr, rw_wk, rw_wv, rw_wo, rw_w0, rw_w1, rw_w2, rw_a0, rw_a1, rw_a2, rw_v0, rw_v1, rw_v2, rw_g1, rw_g2, rw_kk, rw_ka, rw_rk, rw_gn_g, rw_gn_b, pool_w, pool_scale, ffn_w_in, ffn_w_down):
    bp, seq, d = x_prompt.shape
    bs, steps, _ = x_sample.shape
    depth = ln_g.shape[0]
    n_mixers = 2
    alpha = float((2 * depth) ** 0.25)
    m_sample = bs * steps
    nblk = bs // SEQ_PER_CHUNK
    assert steps * SEQ_PER_CHUNK == CHUNK and bs % SEQ_PER_CHUNK == 0
    assert seq % WKV_ROWS == 0 and seq % TM == 0 and m_sample % TM == 0

    def sample_to_rows(a):
        return a.reshape(nblk, SEQ_PER_CHUNK, steps, d).transpose(0, 2, 1, 3).reshape(m_sample, d)

    def rows_to_sample(a):
        return a.reshape(nblk, steps, SEQ_PER_CHUNK, d).transpose(0, 2, 1, 3).reshape(bs, steps, d)

    wo, w_pool = rw_wo.astype(BF16), pool_w.astype(BF16)

    xp = x_prompt.reshape(bp * seq, d)
    xs = sample_to_rows(x_sample)
    xpb = xsb = None
    vf_p = vf_s = None
    new_wkv_p, new_wkv_s, new_shift_p, new_shift_s, new_pool_p, sample_pool_rows = [], [], [], [], [], []
    for i in range(depth):
        j = i // n_mixers
        lg, lb = ln_g[i, 0][None, :], ln_b[i, 0][None, :]
        xs4 = xs.reshape(nblk, steps, SEQ_PER_CHUNK, d)
        if i % n_mixers == 0:
            prev_s = jnp.concatenate([state_shift[j].reshape(nblk, 1, SEQ_PER_CHUNK, d), xs4[:, :-1]],
                                     axis=1).reshape(m_sample, d)
            new_shift_p.append(jnp.concatenate([xp[(b + 1) * seq - 1:(b + 1) * seq] for b in range(bp)]))
            new_shift_s.append(xs4[:, -1].reshape(bs, d))

            w1, w2 = _pad_lora(rw_w1[j], rw_w2[j])
            a1, a2 = _pad_lora(rw_a1[j], rw_a2[j])
            g1, g2 = rw_g1[j].astype(BF16), rw_g2[j].astype(BF16)
            if j == 0:
                params = (rw_mu[j], rw_w0[j][None, :], rw_a0[j][None, :], w1, w2, a1, a2, g1, g2)
            else:
                v1, v2 = _pad_lora(rw_v1[j - 1], rw_v2[j - 1])
                params = (rw_mu[j], rw_w0[j][None, :], rw_a0[j][None, :], rw_v0[j - 1][None, :],
                          w1, w2, a1, a2, v1, v2, g1, g2)
            chan = (rw_kk[j][None, :], rw_ka[j][None, :], rw_rk[j].reshape(1, d),
                    rw_gn_g[j][None, :], rw_gn_b[j][None, :])

            mixed = _rwkv_mix(xs, prev_s, seq, params)
            r, k, v, wr, wk, wv = _rkv_proj_cast(mixed[0], mixed[1], mixed[2], rw_wr, rw_wk, rw_wv, j)
            tok_s, vg_s = (r, k, v) + tuple(mixed[3:6]), (mixed[6] if j > 0 else None)
            mixed = _rwkv_mix(xp, None, seq, params)
            r, k, v = _rkv_proj(mixed[0], mixed[1], mixed[2], wr, wk, wv)
            tok_p, vg_p = (r, k, v) + tuple(mixed[3:6]), (mixed[6] if j > 0 else None)
            if j == 0:
                vf_p, vf_s = tok_p[2], tok_s[2]
            zp, s_p = _wkv_prompt(tok_p, None if j == 0 else (vf_p, vg_p), chan, bp, seq)
            zs, s_s = _wkv_sample(tok_s, None if j == 0 else (vf_s, vg_s), chan, state_wkv[j])
            new_wkv_p.append(s_p)
            new_wkv_s.append(s_s)
            xp, xpb = _out_ln(alpha, zp, xp, wo, j, lg, lb)
            xs, xsb = _out_ln(alpha, zs, xs, wo, j, lg, lb)
        else:
            xs_tb = xs4.transpose(1, 0, 2, 3).reshape(steps, bs, d)
            new_pool_p.append(jnp.stack([xp[(b + 1) * seq - POOL_BUF:(b + 1) * seq] for b in range(bp)]))
            sample_pool_rows.append(xs_tb.transpose(1, 0, 2))
            sc = pool_scale[j][None, :]
            ext = jnp.concatenate([state_pool[j].transpose(1, 0, 2), xs_tb], axis=0)
            h_tb = _pool_sample(ext, w_pool, j, sc)
            h_rows = h_tb.reshape(steps, nblk, SEQ_PER_CHUNK, d).transpose(1, 0, 2, 3).reshape(m_sample, d)
            xp, xpb = _pool_prompt(alpha, xp, w_pool, j, sc, lg, lb, seq)
            xs, xsb = _add_ln(alpha, xs, h_rows, lg, lb)
        lg, lb = ln_g[i, 1][None, :], ln_b[i, 1][None, :]
        xs, wg, wu, wd = _ffn_cast(alpha, xsb, xs, ffn_w_in, ffn_w_down, i, lg, lb)
        xp = _ffn(alpha, xpb, xp, wg, wu, wd, lg, lb)

    new_pool_s = jnp.concatenate([state_pool[:, :, steps:], jnp.stack(sample_pool_rows)], axis=2)
    return (xp.reshape(bp, seq, d), rows_to_sample(xs), jnp.stack(new_wkv_p), jnp.stack(new_shift_p),
            jnp.stack(new_pool_p), jnp.stack(new_wkv_s), jnp.stack(new_shift_s), new_pool_s)
```

```python
import functools
import math

import jax
import jax.numpy as jnp
from jax import lax
from jax.experimental import pallas as pl
from jax.experimental.pallas import tpu as pltpu

F32 = jnp.float32
BF16 = jnp.bfloat16

HEAD = 64
PAIR = 2 * HEAD
CHUNK = 64
SEQ_PER_CHUNK = 16
GN_EPS = 64e-5
LN_EPS = 1e-5
POOL_WINDOWS = (2, 4, 8, 16)
POOL_BUF = 15
HALO = 16
SHIFT_HALO = 8
LORA_PAD = 128

TM = 512
TM_MIX = 256
TM_FFN = 1024
TM_RKV = 1024
TN = 512
TN_CAST = 256
TF = 512
TF_CAST = 256
FFN_DOWN_CHUNK = 512
FFN_HIDDEN_CHUNK = 512
WKV_ROWS = 256
WKV_PAIRS = 16
WKV_SAMPLE_PAIRS = 4
VMEM_LIMIT = 56 * 1024 * 1024


def _cparams(*sem):
    return pltpu.CompilerParams(dimension_semantics=sem, vmem_limit_bytes=VMEM_LIMIT)


def _mm(a, b):
    return jnp.dot(a, b, preferred_element_type=F32)


def _mm_nt(a, b):
    return lax.dot_general(a, b, (((1,), (1,)), ((), ())), preferred_element_type=F32)


def _mm_tn(a, b):
    return lax.dot_general(a, b, (((0,), (0,)), ((), ())), preferred_element_type=F32)


def _sigmoid(x):
    return 1.0 / (1.0 + jnp.exp(-x))


def _layer_norm(v, g, b):
    mu = jnp.mean(v, axis=-1, keepdims=True)
    c = v - mu
    var = jnp.mean(c * c, axis=-1, keepdims=True)
    return c * lax.rsqrt(var + LN_EPS) * g + b


def _split2(x):
    hi = x.astype(BF16)
    return hi, (x - hi.astype(F32)).astype(BF16)


def _row(a):
    return pl.BlockSpec((1, a.shape[-1]), lambda *_: (0, 0))


def _shifted_rows(x, halo_ref, tile_in_seq):
    before = jnp.where(tile_in_seq == 0, 0.0, halo_ref[SHIFT_HALO - 1:SHIFT_HALO, :])
    first_row = lax.broadcasted_iota(jnp.int32, x.shape, 0) == 0
    return jnp.where(first_row, before, pltpu.roll(x, 1, 0))


def _shift_halo_spec(rows, d, tile_index):
    return pl.BlockSpec((SHIFT_HALO, d), lambda *g: (jnp.maximum(tile_index(*g) * (rows // SHIFT_HALO) - 1, 0), 0))


def _mix_kernel(has_vres, tiles_per_seq, *refs):
    refs = list(refs)
    x_ref, prev_ref, mu_ref, w0_ref, a0_ref = refs[:5]
    del refs[:5]
    v0_ref = refs.pop(0) if has_vres else None
    w1_ref, w2_ref, a1_ref, a2_ref = refs[:4]
    del refs[:4]
    v1_ref, v2_ref = (refs.pop(0), refs.pop(0)) if has_vres else (None, None)
    g1_ref, g2_ref = refs[:2]
    del refs[:2]
    if tiles_per_seq is None:
        xr_ref, xk_ref, xv_ref = refs[:3]
        del refs[:3]
    lw_ref, ag_ref, g_ref = refs[:3]
    vg_ref = refs[3] if has_vres else None

    x = x_ref[...]
    if tiles_per_seq is None:
        xprev = prev_ref[...]
    else:
        xprev = _shifted_rows(x, prev_ref, pl.program_id(0) % tiles_per_seq)
    xx = xprev - x

    def mix(j):
        return (x + xx * mu_ref[j:j + 1, :]).astype(BF16)

    if tiles_per_seq is None:
        xr_ref[...] = mix(0)
        xk_ref[...] = mix(2)
        xv_ref[...] = mix(3)

    u = w0_ref[...] + _mm(jnp.tanh(_mm(mix(1), w1_ref[...])).astype(BF16), w2_ref[...])
    z = -u
    softplus = jnp.maximum(z, 0.0) + jnp.log(1.0 + jnp.exp(-jnp.abs(z)))
    lw_ref[...] = -jnp.exp(-softplus - 0.5)
    ag_ref[...] = _sigmoid(a0_ref[...] + _mm(_mm(mix(4), a1_ref[...]).astype(BF16), a2_ref[...]))
    g_ref[...] = _mm(_sigmoid(_mm(mix(5), g1_ref[...])).astype(BF16), g2_ref[...]).astype(g_ref.dtype)
    if has_vres:
        vg_ref[...] = _sigmoid(v0_ref[...] + _mm(_mm(mix(3), v1_ref[...]).astype(BF16), v2_ref[...]))


def _rwkv_mix(x, xprev, seq, params):
    m, d = x.shape
    has_vres = len(params) == 12
    tile = pl.BlockSpec((TM_MIX, d), lambda i: (i, 0))
    if xprev is None:
        prev, prev_spec, tiles_per_seq = x, _shift_halo_spec(TM_MIX, d, lambda i: i), seq // TM_MIX
        out_shape = []
    else:
        prev, prev_spec, tiles_per_seq = xprev, tile, None
        out_shape = [jax.ShapeDtypeStruct((m, d), BF16)] * 3

    def full(a):
        return pl.BlockSpec(a.shape, lambda i: (0,) * a.ndim)

    out_shape += [jax.ShapeDtypeStruct((m, d), F32)] * 2 + [jax.ShapeDtypeStruct((m, d), BF16)]
    if has_vres:
        out_shape.append(jax.ShapeDtypeStruct((m, d), F32))
    return pl.pallas_call(
        functools.partial(_mix_kernel, has_vres, tiles_per_seq),
        grid=(m // TM_MIX,),
        in_specs=[tile, prev_spec] + [full(a) for a in params],
        out_specs=[tile] * len(out_shape),
        out_shape=out_shape,
        compiler_params=_cparams("parallel"),
        name="rwkv_mix",
    )(x, prev, *params)


def _rkv_kernel(tiles_per_seq, x_ref, halo_ref, mu_ref, wr_ref, wk_ref, wv_ref, r_ref, k_ref, v_ref):
    x = x_ref[...]
    xx = _shifted_rows(x, halo_ref, pl.program_id(1) % tiles_per_seq) - x
    for j, w_ref, o_ref in ((0, wr_ref, r_ref), (2, wk_ref, k_ref), (3, wv_ref, v_ref)):
        o_ref[...] = _mm((x + xx * mu_ref[j:j + 1, :]).astype(BF16), w_ref[...])


def _rkv_proj(x, mu, wr, wk, wv, seq):
    m, d = x.shape
    xs = pl.BlockSpec((TM_RKV, d), lambda n, i: (i, 0))
    ws = pl.BlockSpec((d, TN), lambda n, i: (0, n))
    os = pl.BlockSpec((TM_RKV, TN), lambda n, i: (i, n))
    return pl.pallas_call(
        functools.partial(_rkv_kernel, seq // TM_RKV),
        grid=(d // TN, m // TM_RKV),
        in_specs=[xs, _shift_halo_spec(TM_RKV, d, lambda n, i: i),
                  pl.BlockSpec(mu.shape, lambda n, i: (0, 0)), ws, ws, ws],
        out_specs=[os, os, os],
        out_shape=[jax.ShapeDtypeStruct((m, d), F32)] * 3,
        compiler_params=_cparams("parallel", "arbitrary"),
        name="rkv_proj",
    )(x, x, mu, wr, wk, wv)


def _rkv_cast_kernel(xr_ref, xk_ref, xv_ref, wr_ref, wk_ref, wv_ref, r_ref, k_ref, v_ref, wrb_ref, wkb_ref, wvb_ref):
    for x_ref, w_ref, o_ref, wb_ref in ((xr_ref, wr_ref, r_ref, wrb_ref), (xk_ref, wk_ref, k_ref, wkb_ref),
                                        (xv_ref, wv_ref, v_ref, wvb_ref)):
        wb = w_ref[...].astype(BF16)
        wb_ref[...] = wb
        o_ref[...] = _mm(x_ref[...], wb)


def _rkv_proj_cast(xr, xk, xv, wr, wk, wv, layer):
    m, d = xr.shape
    assert m == TM
    xs = pl.BlockSpec((TM, d), lambda n: (0, 0))
    ws = pl.BlockSpec((None, d, TN_CAST), lambda n: (layer, 0, n))
    os = pl.BlockSpec((TM, TN_CAST), lambda n: (0, n))
    wbs = pl.BlockSpec((d, TN_CAST), lambda n: (0, n))
    return pl.pallas_call(
        _rkv_cast_kernel,
        grid=(d // TN_CAST,),
        in_specs=[xs, xs, xs, ws, ws, ws],
        out_specs=[os, os, os, wbs, wbs, wbs],
        out_shape=[jax.ShapeDtypeStruct((m, d), F32)] * 3 + [jax.ShapeDtypeStruct((d, d), BF16)] * 3,
        compiler_params=_cparams("parallel"),
        name="rkv_proj_cast",
    )(xr, xk, xv, wr, wk, wv)


class _ChunkMasks:
    def __init__(self, sample):
        row = lax.broadcasted_iota(jnp.int32, (CHUNK, PAIR), 0)
        lane = lax.broadcasted_iota(jnp.int32, (CHUNK, PAIR), 1)
        col = lane & (CHUNK - 1)
        r2 = lax.broadcasted_iota(jnp.int32, (CHUNK, CHUNK), 0)
        c2 = lax.broadcasted_iota(jnp.int32, (CHUNK, CHUNK), 1)
        if sample:
            shift = int(math.log2(SEQ_PER_CHUNK))
            seq_mask = SEQ_PER_CHUNK - 1
            same = (row & seq_mask) == (col & seq_mask)
            self.strict = same & ((col >> shift) < (row >> shift))
            self.incl = same & ((col >> shift) <= (row >> shift))
            tri = ((r2 & seq_mask) == (c2 & seq_mask)) & ((c2 >> shift) <= (r2 >> shift))
            self.squarings = 1
        else:
            self.strict = col < row
            self.incl = col <= row
            tri = c2 <= r2
            self.squarings = 5
        self.sample = sample
        self.eye = jnp.where(row == col, 1.0, 0.0).astype(F32)
        self.tri = jnp.where(tri, 1.0, 0.0).astype(BF16)
        self.head0 = lane < HEAD
        r128 = lax.broadcasted_iota(jnp.int32, (PAIR, PAIR), 0)
        c128 = lax.broadcasted_iota(jnp.int32, (PAIR, PAIR), 1)
        self.same_head = (r128 < HEAD) == (c128 < HEAD)
        self.diag128 = r128 == c128


def _blk(y, cm):
    return jnp.concatenate([jnp.where(cm.head0, y, 0.0), jnp.where(cm.head0, 0.0, y)], axis=0).astype(BF16)


def _headsum(xs, cm):
    parts = [(jnp.sum(jnp.where(cm.head0, x, 0.0), axis=1, keepdims=True),
              jnp.sum(jnp.where(cm.head0, 0.0, x), axis=1, keepdims=True)) for x in xs]
    return [jnp.where(cm.head0, s0, s1) for s0, s1 in parts]


def _wkv_pre(kraw, v, ag, vf, vg, kkw, ka, cm):
    kkv = [k * w for k, w in zip(kraw, kkw)]
    sumsq = _headsum([x * x for x in kkv], cm)
    kkn = [x / jnp.maximum(jnp.sqrt(s), 1e-12) for x, s in zip(kkv, sumsq)]
    kmod = [k * (1.0 + (g - 1.0) * c) for k, g, c in zip(kraw, ag, ka)]
    if vf is not None:
        v = [x + (f - x) * g for x, f, g in zip(v, vf, vg)]
    return kmod, v, [-x for x in kkn], [x * g for x, g in zip(kkn, ag)]


def _wkv_post(y, r, kmod, v, g, rk, gng, gnb, cm):
    inv_n = 1.0 / HEAD
    n = len(y)
    sums = _headsum(list(y) + [a * b * c for a, b, c in zip(r, kmod, rk)], cm)
    yc = [a - s * inv_n for a, s in zip(y, sums[:n])]
    var = _headsum([c * c for c in yc], cm)
    return [((c * lax.rsqrt(s * inv_n + GN_EPS) * gg + gb + bs * vv) * gt).astype(BF16)
            for c, s, gg, gb, bs, vv, gt in zip(yc, var, gng, gnb, sums[n:], v, g)]


def _chunk_prep(r, lw, k, v, a, b, cm):
    splits = [_split2(x) for x in lw]
    cum = [_mm(cm.tri, hi) + _mm(cm.tri, lo) for hi, lo in splits]
    if cm.sample:
        cl = [jnp.concatenate([c[CHUNK - SEQ_PER_CHUNK:, :]] * (CHUNK // SEQ_PER_CHUNK), axis=0) for c in cum]
    else:
        cl = [jnp.broadcast_to(c[CHUNK - 1:CHUNK, :], c.shape) for c in cum]
    w_inv = [jnp.exp(-c) for c in cum]
    w_last = [jnp.exp(c) for c in cl]
    rt = [x * jnp.exp(c) for x, c in zip(r, cum)]
    at = [x * jnp.exp(c - l) for x, c, l in zip(a, cum, lw)]
    bt = [x * w for x, w in zip(b, w_inv)]
    kt = [x * w for x, w in zip(k, w_inv)]
    x = [jnp.concatenate([p, q], axis=0).astype(BF16) for p, q in zip(at, rt)]
    blk_bk = [jnp.concatenate([_blk(p, cm), _blk(q, cm)], axis=0) for p, q in zip(bt, kt)]
    prod = [_mm_nt(p, q) for p, q in zip(x, blk_bk)]
    nak = [jnp.where(cm.strict, y[:CHUNK, PAIR:], 0.0).astype(BF16) for y in prod]
    mrbk = [jnp.concatenate([jnp.where(cm.incl, y[CHUNK:, :PAIR], 0.0),
                             jnp.where(cm.incl, y[CHUNK:, PAIR:], 0.0)], axis=1).astype(BF16) for y in prod]
    p = [jnp.where(cm.strict, y[:CHUNK, :PAIR], 0.0) for y in prod]
    t = [cm.eye + y for y in p]
    p = [_mm(y.astype(BF16), _blk(y, cm)) for y in p]
    for _ in range(cm.squarings - 1):
        both = [_mm(jnp.concatenate([y, z], axis=0).astype(BF16), _blk(y, cm)) for y, z in zip(p, t)]
        p = [y[:CHUNK] for y in both]
        t = [z + y[CHUNK:] for z, y in zip(t, both)]
    t = [z + _mm(z.astype(BF16), _blk(y, cm)) for z, y in zip(t, p)]
    tb = [y.astype(BF16) for y in t]
    blk_v = [_blk(y, cm) for y in v]
    q = [_mm(m, w) for m, w in zip(nak, blk_v)]
    au = [_mm(m, jnp.concatenate([_blk(y, cm), _blk(z, cm)], axis=1)) for m, y, z in zip(tb, at, q)]
    ah = [y[:, :PAIR] for y in au]
    uh = [y[:, PAIR:] for y in au]
    rh = [y + _mm(m[:, :PAIR], _blk(z, cm)) for y, m, z in zip(rt, mrbk, ah)]
    yh = [_mm(m, jnp.concatenate([_blk(z, cm), w], axis=0)) for m, z, w in zip(mrbk, uh, blk_v)]
    bh = [y * w for y, w in zip(bt, w_last)]
    kh = [y * w for y, w in zip(kt, w_last)]
    return ah, uh, rh, yh, bh, kh, w_last


def _wkv_prompt_kernel(has_vres, n_chunks, n_pairs, *refs):
    if has_vres:
        (r_ref, k_ref, v_ref, lw_ref, ag_ref, g_ref, vf_ref, vg_ref,
         kkw_ref, ka_ref, rk_ref, gng_ref, gnb_ref, z_ref, s_ref, s_scr) = refs
    else:
        (r_ref, k_ref, v_ref, lw_ref, ag_ref, g_ref,
         kkw_ref, ka_ref, rk_ref, gng_ref, gnb_ref, z_ref, s_ref, s_scr) = refs
    step = pl.program_id(1)

    @pl.when(step == 0)
    def _():
        s_scr[...] = jnp.zeros_like(s_scr)

    cm = _ChunkMasks(sample=False)
    lanes = [slice(p * PAIR, (p + 1) * PAIR) for p in range(n_pairs)]

    def chunk(c, carry):
        rows = pl.ds(pl.multiple_of(c * CHUNK, CHUNK), CHUNK)

        def tok(ref):
            return [ref[rows, l] for l in lanes]

        def chan(ref):
            return [ref[:, l] for l in lanes]

        r = tok(r_ref)
        kmod, v, a, b = _wkv_pre(tok(k_ref), tok(v_ref), tok(ag_ref),
                                 tok(vf_ref) if has_vres else None, tok(vg_ref) if has_vres else None,
                                 chan(kkw_ref), chan(ka_ref), cm)
        ah, uh, rh, yh, bh, kh, w_last = _chunk_prep(r, tok(lw_ref), kmod, v, a, b, cm)
        s16 = [s_scr[p].astype(BF16) for p in range(n_pairs)]
        y = [_mm_nt(x.astype(BF16), h) + z for x, h, z in zip(rh, s16, yh)]
        bh16 = [x.astype(BF16) for x in bh]
        ab = [_mm_tn(x.astype(BF16), w) for x, w in zip(ah, bh16)]
        uvbk = [_mm_tn(jnp.concatenate([p, q], axis=0).astype(BF16),
                       jnp.concatenate([w, x.astype(BF16)], axis=0))
                for p, q, w, x in zip(uh, v, bh16, kh)]
        phi = [(jnp.where(cm.diag128, jnp.broadcast_to(w[:1, :], (PAIR, PAIR)), 0.0)
                + jnp.where(cm.same_head, m, 0.0)).astype(BF16) for w, m in zip(w_last, ab)]
        s_new = [_mm(h, f) + jnp.where(cm.same_head, m, 0.0) for h, f, m in zip(s16, phi, uvbk)]
        for p in range(n_pairs):
            s_scr[p] = s_new[p]
        z = _wkv_post(y, r, kmod, v, tok(g_ref), chan(rk_ref), chan(gng_ref), chan(gnb_ref), cm)
        for l, zz in zip(lanes, z):
            z_ref[rows, l] = zz
        return carry

    lax.fori_loop(0, n_chunks, chunk, 0)

    @pl.when(step == pl.num_programs(1) - 1)
    def _():
        for p in range(n_pairs):
            s = s_scr[p]
            s_ref[0, 2 * p] = s[:HEAD, :HEAD]
            s_ref[0, 2 * p + 1] = s[HEAD:, HEAD:]


def _wkv_prompt(tok, vres, chan, batch, seq):
    m, d = tok[0].shape
    has_vres = vres is not None
    assert d == WKV_PAIRS * PAIR
    steps = seq // WKV_ROWS
    tile = pl.BlockSpec((WKV_ROWS, d), lambda b, s: (b * steps + s, 0))
    args = tuple(tok) + (tuple(vres) if has_vres else ()) + tuple(chan)
    n_tok = len(tok) + (2 if has_vres else 0)
    n_heads = d // HEAD
    return pl.pallas_call(
        functools.partial(_wkv_prompt_kernel, has_vres, WKV_ROWS // CHUNK, WKV_PAIRS),
        grid=(batch, steps),
        in_specs=[tile] * n_tok + [_row(c) for c in chan],
        out_specs=[tile, pl.BlockSpec((1, n_heads, HEAD, HEAD), lambda b, s: (b, 0, 0, 0))],
        out_shape=[jax.ShapeDtypeStruct((m, d), BF16),
                   jax.ShapeDtypeStruct((batch, n_heads, HEAD, HEAD), F32)],
        scratch_shapes=[pltpu.VMEM((WKV_PAIRS, PAIR, PAIR), F32)],
        compiler_params=_cparams("parallel", "arbitrary"),
        name="wkv_prompt",
    )(*args)


def _wkv_sample_kernel(has_vres, n_pairs, *refs):
    if has_vres:
        (r_ref, k_ref, v_ref, lw_ref, ag_ref, g_ref, vf_ref, vg_ref,
         kkw_ref, ka_ref, rk_ref, gng_ref, gnb_ref, s_in_ref, z_ref, s_out_ref,
         x_scr, u_scr, bk_scr, gy_scr) = refs
    else:
        (r_ref, k_ref, v_ref, lw_ref, ag_ref, g_ref,
         kkw_ref, ka_ref, rk_ref, gng_ref, gnb_ref, s_in_ref, z_ref, s_out_ref,
         x_scr, u_scr, bk_scr, gy_scr) = refs
    cm = _ChunkMasks(sample=True)
    row8 = lax.broadcasted_iota(jnp.int32, (8, PAIR), 0)
    zeros_h = jnp.zeros((HEAD, HEAD), F32)
    steps = CHUNK // SEQ_PER_CHUNK
    lanes = [slice(p * PAIR, (p + 1) * PAIR) for p in range(n_pairs)]

    def tok(ref):
        return [ref[:, l] for l in lanes]

    r = tok(r_ref)
    kmod, v, a, b = _wkv_pre(tok(k_ref), tok(v_ref), tok(ag_ref),
                             tok(vf_ref) if has_vres else None, tok(vg_ref) if has_vres else None,
                             tok(kkw_ref), tok(ka_ref), cm)
    ah, uh, rh, yh, bh, kh, w_last = _chunk_prep(r, tok(lw_ref), kmod, v, a, b, cm)
    for p in range(n_pairs):
        x_scr[p, 0:CHUNK, :] = ah[p]
        x_scr[p, CHUNK:, :] = rh[p]
        u_scr[p, 0:CHUNK, :] = uh[p]
        u_scr[p, CHUNK:, :] = yh[p]
        bk_scr[p, 0:CHUNK, :] = bh[p]
        bk_scr[p, CHUNK:, :] = kh[p]
        gy_scr[p, CHUNK:, :] = v[p]

    seqs = range(SEQ_PER_CHUNK)
    for p in range(n_pairs):
        pick = [pl.ds(i, 2 * steps, stride=SEQ_PER_CHUNK) for i in seqs]
        s = [jnp.concatenate([jnp.concatenate([s_in_ref[i, 2 * p], zeros_h], axis=1),
                              jnp.concatenate([zeros_h, s_in_ref[i, 2 * p + 1]], axis=1)], axis=0) for i in seqs]
        uy = [_mm_nt(x_scr[p, pk, :].astype(BF16), m.astype(BF16)) + u_scr[p, pk, :] for pk, m in zip(pick, s)]
        uv = [jnp.where(row8 < steps, m, gy_scr[p, pk, :]) for pk, m in zip(pick, uy)]
        for pk, m in zip(pick, uy):
            gy_scr[p, pk, :] = m
        upd = [_mm_tn(m.astype(BF16), bk_scr[p, pk, :].astype(BF16)) for pk, m in zip(pick, uv)]
        for i in seqs:
            s_new = s[i] * w_last[p][i:i + 1, :] + jnp.where(cm.same_head, upd[i], 0.0)
            s_out_ref[i, 2 * p] = s_new[:HEAD, :HEAD]
            s_out_ref[i, 2 * p + 1] = s_new[HEAD:, HEAD:]

    y = [gy_scr[p, CHUNK:, :] for p in range(n_pairs)]
    z = _wkv_post(y, r, kmod, v, tok(g_ref), tok(rk_ref), tok(gng_ref), tok(gnb_ref), cm)
    for l, zz in zip(lanes, z):
        z_ref[:, l] = zz


def _wkv_sample(tok, vres, chan, state):
    m, d = tok[0].shape
    nb = state.shape[0]
    has_vres = vres is not None
    n_pairs = WKV_SAMPLE_PAIRS
    width = n_pairs * PAIR
    tile = pl.BlockSpec((CHUNK, width), lambda i, p: (i, p))
    row = pl.BlockSpec((1, width), lambda i, p: (0, p))
    blk = (SEQ_PER_CHUNK, 2 * n_pairs, HEAD, HEAD)
    args = tuple(tok) + (tuple(vres) if has_vres else ()) + tuple(chan) + (state,)
    n_tok = len(tok) + (2 if has_vres else 0)
    return pl.pallas_call(
        functools.partial(_wkv_sample_kernel, has_vres, n_pairs),
        grid=(nb // SEQ_PER_CHUNK, d // width),
        in_specs=[tile] * n_tok + [row] * len(chan) + [pl.BlockSpec(blk, lambda i, p: (i, p, 0, 0))],
        out_specs=[tile, pl.BlockSpec(blk, lambda i, p: (i, p, 0, 0))],
        out_shape=[jax.ShapeDtypeStruct((m, d), BF16), jax.ShapeDtypeStruct(state.shape, F32)],
        scratch_shapes=[pltpu.VMEM((n_pairs, 2 * CHUNK, PAIR), F32)] * 4,
        compiler_params=_cparams("parallel", "parallel"),
        name="wkv_sample",
    )(*args)


def _out_ln_kernel(alpha, z_ref, x_ref, w_ref, g_ref, b_ref, o_ref, ob_ref):
    y = _layer_norm(alpha * x_ref[...] + _mm(z_ref[...], w_ref[...]), g_ref[...], b_ref[...])
    o_ref[...] = y
    ob_ref[...] = y.astype(BF16)


def _out_ln(alpha, z, x, w, layer, g, b):
    m, d = x.shape
    tile = pl.BlockSpec((TM, d), lambda i: (i, 0))
    return pl.pallas_call(
        functools.partial(_out_ln_kernel, alpha),
        grid=(m // TM,),
        in_specs=[tile, tile, pl.BlockSpec((None, d, d), lambda i: (layer, 0, 0)), _row(g), _row(b)],
        out_specs=[tile, tile],
        out_shape=[jax.ShapeDtypeStruct((m, d), F32), jax.ShapeDtypeStruct((m, d), BF16)],
        compiler_params=_cparams("parallel"),
        name="out_ln",
    )(z, x, w, g, b)


def _ffn_kernel(alpha, emit_weights, *refs):
    if emit_weights:
        (xb_ref, x_hbm, wg_ref, wu_ref, wd_ref, g_ref, b_ref,
         o_ref, wgb_ref, wub_ref, wdb_ref, xres_ref, sem) = refs
    else:
        xb_ref, x_hbm, wg_ref, wu_ref, wd_ref, g_ref, b_ref, o_ref, xres_ref, sem = refs
    f = pl.program_id(1)
    rows = o_ref.shape[0]

    def residual_copy():
        start = pl.multiple_of(pl.program_id(0) * rows, rows)
        return pltpu.make_async_copy(x_hbm.at[pl.ds(start, rows), :], xres_ref, sem)

    @pl.when(f == 0)
    def _():
        residual_copy().start()
        o_ref[...] = jnp.zeros_like(o_ref)

    if emit_weights:
        wgb_ref[...] = wg_ref[...].astype(BF16)
        wub_ref[...] = wu_ref[...].astype(BF16)
        wdb_ref[...] = wd_ref[...].astype(BF16)
        wg_ref, wu_ref, wd_ref = wgb_ref, wub_ref, wdb_ref
    xb = xb_ref[...]
    for h in range(0, wg_ref.shape[1], FFN_HIDDEN_CHUNK):
        hid = slice(h, h + FFN_HIDDEN_CHUNK)
        gate = _mm(xb, wg_ref[:, hid])
        up = _mm(xb, wu_ref[:, hid])
        act = (gate * _sigmoid(gate) * up).astype(BF16)
        for c in range(0, o_ref.shape[1], FFN_DOWN_CHUNK):
            cols = slice(c, c + FFN_DOWN_CHUNK)
            o_ref[:, cols] += _mm(act, wd_ref[hid, cols])

    @pl.when(f == pl.num_programs(1) - 1)
    def _():
        residual_copy().wait()
        o_ref[...] = _layer_norm(alpha * xres_ref[...] + o_ref[...], g_ref[...], b_ref[...])


def _ffn(alpha, xb, x, wg, wu, wd, g, b):
    m, d = x.shape
    d_ff = wd.shape[0]
    tile = pl.BlockSpec((TM_FFN, d), lambda i, f: (i, 0))
    in_tile = pl.BlockSpec((TM_FFN, d), lambda i, f: (i, 0), pipeline_mode=pl.Buffered(1))
    return pl.pallas_call(
        functools.partial(_ffn_kernel, alpha, False),
        grid=(m // TM_FFN, d_ff // TF),
        in_specs=[in_tile, pl.BlockSpec(memory_space=pl.ANY),
                  pl.BlockSpec((d, TF), lambda i, f: (0, f)),
                  pl.BlockSpec((d, TF), lambda i, f: (0, f)),
                  pl.BlockSpec((TF, d), lambda i, f: (f, 0)),
                  _row(g), _row(b)],
        out_specs=tile,
        out_shape=jax.ShapeDtypeStruct((m, d), F32),
        scratch_shapes=[pltpu.VMEM((TM_FFN, d), F32), pltpu.SemaphoreType.DMA(())],
        compiler_params=_cparams("parallel", "arbitrary"),
        name="ffn",
    )(xb, x, wg, wu, wd, g, b)


def _ffn_cast(alpha, xb, x, w_in, w_down, layer, g, b):
    m, d = x.shape
    assert m == TM
    d_ff = w_down.shape[1]
    nf = d_ff // TF_CAST
    tile = pl.BlockSpec((TM, d), lambda i, f: (i, 0))
    return pl.pallas_call(
        functools.partial(_ffn_kernel, alpha, True),
        grid=(1, nf),
        in_specs=[tile, pl.BlockSpec(memory_space=pl.ANY),
                  pl.BlockSpec((None, d, TF_CAST), lambda i, f: (layer, 0, f)),
                  pl.BlockSpec((None, d, TF_CAST), lambda i, f: (layer, 0, f + nf)),
                  pl.BlockSpec((None, TF_CAST, d), lambda i, f: (layer, f, 0)),
                  _row(g), _row(b)],
        out_specs=[tile,
                   pl.BlockSpec((d, TF_CAST), lambda i, f: (0, f)),
                   pl.BlockSpec((d, TF_CAST), lambda i, f: (0, f)),
                   pl.BlockSpec((TF_CAST, d), lambda i, f: (f, 0))],
        out_shape=[jax.ShapeDtypeStruct((m, d), F32),
                   jax.ShapeDtypeStruct((d, d_ff), BF16), jax.ShapeDtypeStruct((d, d_ff), BF16),
                   jax.ShapeDtypeStruct((d_ff, d), BF16)],
        scratch_shapes=[pltpu.VMEM((TM, d), F32), pltpu.SemaphoreType.DMA(())],
        compiler_params=_cparams("arbitrary", "arbitrary"),
        name="ffn_cast",
    )(xb, x, w_in, w_in, w_down, g, b)


def _pool_prompt_kernel(alpha, tiles_per_seq, x_ref, halo_ref, w_ref, sc_ref, g_ref, b_ref, o_ref, ob_ref):
    tile_in_seq = pl.program_id(0) % tiles_per_seq
    x = x_ref[...]
    halo = jnp.where(tile_in_seq == 0, 0.0, halo_ref[...])
    gw = x.shape[1] // len(POOL_WINDOWS)
    pos = tile_in_seq * TM + lax.broadcasted_iota(jnp.int32, (TM, gw), 0)
    outs = []
    for gi, win in enumerate(POOL_WINDOWS):
        lanes = slice(gi * gw, (gi + 1) * gw)
        xg = x[:, lanes]
        s = jnp.concatenate([halo[:, lanes], xg], axis=0)
        span = 1
        while span < win:
            s = s[span:] + s[:-span]
            span *= 2
        first = HALO - (win - 1)
        cnt = jnp.minimum(win, pos + 1).astype(F32)
        dg = s[first:first + TM] / cnt - xg
        outs.append(_mm(dg.astype(BF16), w_ref[gi]))
    h = jnp.concatenate(outs, axis=1) * sc_ref[...]
    y = _layer_norm(alpha * x + h, g_ref[...], b_ref[...])
    o_ref[...] = y
    ob_ref[...] = y.astype(BF16)


def _pool_prompt(alpha, x, w, layer, scale, g, b, seq):
    m, d = x.shape
    tile = pl.BlockSpec((TM, d), lambda i: (i, 0))
    halo = pl.BlockSpec((HALO, d), lambda i: (jnp.maximum(i * (TM // HALO) - 1, 0), 0))
    return pl.pallas_call(
        functools.partial(_pool_prompt_kernel, alpha, seq // TM),
        grid=(m // TM,),
        in_specs=[tile, halo, pl.BlockSpec((None,) + w.shape[1:], lambda i: (layer, 0, 0, 0)),
                  _row(scale), _row(g), _row(b)],
        out_specs=[tile, tile],
        out_shape=[jax.ShapeDtypeStruct((m, d), F32), jax.ShapeDtypeStruct((m, d), BF16)],
        compiler_params=_cparams("parallel"),
        name="pool_prompt",
    )(x, x, w, scale, g, b)


def _pool_sample_kernel(ext_ref, w_ref, sc_ref, h_ref):
    steps = ext_ref.shape[0] - POOL_BUF
    gi = pl.program_id(0)
    ds = []
    for t in range(steps):
        cur = ext_ref[POOL_BUF + t]
        acc16 = cur
        sums = {}
        for i in range(1, max(POOL_WINDOWS)):
            acc16 = acc16 + ext_ref[POOL_BUF + t - i]
            if i + 1 in POOL_WINDOWS:
                sums[i + 1] = acc16
        d = sums[POOL_WINDOWS[-1]] * (1.0 / POOL_WINDOWS[-1])
        for j, win in enumerate(POOL_WINDOWS[:-1]):
            d = jnp.where(gi == j, sums[win] * (1.0 / win), d)
        ds.append((d - cur).astype(BF16))
    h_ref[...] = _mm(jnp.concatenate(ds, axis=0), w_ref[...]) * sc_ref[...]


def _pool_sample(ext, w, layer, scale):
    n, nb, d = ext.shape
    ng = len(POOL_WINDOWS)
    gw = d // ng
    return pl.pallas_call(
        _pool_sample_kernel,
        grid=(ng,),
        in_specs=[pl.BlockSpec((n, nb, gw), lambda gi: (0, 0, gi)),
                  pl.BlockSpec((None, None, gw, gw), lambda gi: (layer, gi, 0, 0)),
                  pl.BlockSpec((1, gw), lambda gi: (0, gi))],
        out_specs=pl.BlockSpec(((n - POOL_BUF) * nb, gw), lambda gi: (0, gi)),
        out_shape=jax.ShapeDtypeStruct(((n - POOL_BUF) * nb, d), F32),
        compiler_params=_cparams("parallel"),
        name="pool_sample",
    )(ext, w, scale)


def _add_ln_kernel(alpha, x_ref, h_ref, g_ref, b_ref, o_ref, ob_ref):
    y = _layer_norm(alpha * x_ref[...] + h_ref[...], g_ref[...], b_ref[...])
    o_ref[...] = y
    ob_ref[...] = y.astype(BF16)


def _add_ln(alpha, x, h, g, b):
    m, d = x.shape
    tile = pl.BlockSpec((TM, d), lambda i: (i, 0))
    return pl.pallas_call(
        functools.partial(_add_ln_kernel, alpha),
        grid=(m // TM,),
        in_specs=[tile, tile, _row(g), _row(b)],
        out_specs=[tile, tile],
        out_shape=[jax.ShapeDtypeStruct((m, d), F32), jax.ShapeDtypeStruct((m, d), BF16)],
        compiler_params=_cparams("parallel"),
        name="add_ln",
    )(x, h, g, b)


def _pad_lora(w_a, w_b):
    rank = w_a.shape[1]
    pad = (-rank) % LORA_PAD
    return (jnp.pad(w_a, ((0, 0), (0, pad))).astype(BF16), jnp.pad(w_b, ((0, pad), (0, 0))).astype(BF16))


def kernel(x_prompt, x_sample, state_wkv, state_shift, state_pool, ln_g, ln_b, rw_mu, rw_wr, rw_wk, rw_wv, rw_wo, rw_w0, rw_w1, rw_w2, rw_a0, rw_a1, rw_a2, rw_v0, rw_v1, rw_v2, rw_g1, rw_g2, rw_kk, rw_ka, rw_rk, rw_gn_g, rw_gn_b, pool_w, pool_scale, ffn_w_in, ffn_w_down):
    bp, seq, d = x_prompt.shape
    bs, steps, _ = x_sample.shape
    depth = ln_g.shape[0]
    n_mixers = 2
    alpha = float((2 * depth) ** 0.25)
    m_sample = bs * steps
    nblk = bs // SEQ_PER_CHUNK
    assert steps * SEQ_PER_CHUNK == CHUNK and bs % SEQ_PER_CHUNK == 0
    assert seq % WKV_ROWS == 0 and seq % TM == 0 and m_sample % TM == 0

    def sample_to_rows(a):
        return a.reshape(nblk, SEQ_PER_CHUNK, steps, d).transpose(0, 2, 1, 3).reshape(m_sample, d)

    def rows_to_sample(a):
        return a.reshape(nblk, steps, SEQ_PER_CHUNK, d).transpose(0, 2, 1, 3).reshape(bs, steps, d)

    wo, w_pool = rw_wo.astype(BF16), pool_w.astype(BF16)

    xp = x_prompt.reshape(bp * seq, d)
    xs = sample_to_rows(x_sample)
    xpb = xsb = None
    vf_p = vf_s = None
    new_wkv_p, new_wkv_s, new_shift_p, new_shift_s, new_pool_p, sample_pool_rows = [], [], [], [], [], []
    for i in range(depth):
        j = i // n_mixers
        lg, lb = ln_g[i, 0][None, :], ln_b[i, 0][None, :]
        xs4 = xs.reshape(nblk, steps, SEQ_PER_CHUNK, d)
        if i % n_mixers == 0:
            prev_s = jnp.concatenate([state_shift[j].reshape(nblk, 1, SEQ_PER_CHUNK, d), xs4[:, :-1]],
                                     axis=1).reshape(m_sample, d)
            new_shift_p.append(jnp.concatenate([xp[(b + 1) * seq - 1:(b + 1) * seq] for b in range(bp)]))
            new_shift_s.append(xs4[:, -1].reshape(bs, d))

            w1, w2 = _pad_lora(rw_w1[j], rw_w2[j])
            a1, a2 = _pad_lora(rw_a1[j], rw_a2[j])
            g1, g2 = rw_g1[j].astype(BF16), rw_g2[j].astype(BF16)
            if j == 0:
                params = (rw_mu[j], rw_w0[j][None, :], rw_a0[j][None, :], w1, w2, a1, a2, g1, g2)
            else:
                v1, v2 = _pad_lora(rw_v1[j - 1], rw_v2[j - 1])
                params = (rw_mu[j], rw_w0[j][None, :], rw_a0[j][None, :], rw_v0[j - 1][None, :],
                          w1, w2, a1, a2, v1, v2, g1, g2)
            chan = (rw_kk[j][None, :], rw_ka[j][None, :], rw_rk[j].reshape(1, d),
                    rw_gn_g[j][None, :], rw_gn_b[j][None, :])

            mixed = _rwkv_mix(xs, prev_s, seq, params)
            r, k, v, wr, wk, wv = _rkv_proj_cast(mixed[0], mixed[1], mixed[2], rw_wr, rw_wk, rw_wv, j)
            tok_s, vg_s = (r, k, v) + tuple(mixed[3:6]), (mixed[6] if j > 0 else None)
            mixed = _rwkv_mix(xp, None, seq, params)
            r, k, v = _rkv_proj(xp, rw_mu[j], wr, wk, wv, seq)
            tok_p, vg_p = (r, k, v) + tuple(mixed[:3]), (mixed[3] if j > 0 else None)
            if j == 0:
                vf_p, vf_s = tok_p[2], tok_s[2]
            zp, s_p = _wkv_prompt(tok_p, None if j == 0 else (vf_p, vg_p), chan, bp, seq)
            zs, s_s = _wkv_sample(tok_s, None if j == 0 else (vf_s, vg_s), chan, state_wkv[j])
            new_wkv_p.append(s_p)
            new_wkv_s.append(s_s)
            xp, xpb = _out_ln(alpha, zp, xp, wo, j, lg, lb)
            xs, xsb = _out_ln(alpha, zs, xs, wo, j, lg, lb)
        else:
            xs_tb = xs4.transpose(1, 0, 2, 3).reshape(steps, bs, d)
            new_pool_p.append(jnp.stack([xp[(b + 1) * seq - POOL_BUF:(b + 1) * seq] for b in range(bp)]))
            sample_pool_rows.append(xs_tb.transpose(1, 0, 2))
            sc = pool_scale[j][None, :]
            ext = jnp.concatenate([state_pool[j].transpose(1, 0, 2), xs_tb], axis=0)
            h_tb = _pool_sample(ext, w_pool, j, sc)
            h_rows = h_tb.reshape(steps, nblk, SEQ_PER_CHUNK, d).transpose(1, 0, 2, 3).reshape(m_sample, d)
            xp, xpb = _pool_prompt(alpha, xp, w_pool, j, sc, lg, lb, seq)
            xs, xsb = _add_ln(alpha, xs, h_rows, lg, lb)
        lg, lb = ln_g[i, 1][None, :], ln_b[i, 1][None, :]
        xs, wg, wu, wd = _ffn_cast(alpha, xsb, xs, ffn_w_in, ffn_w_down, i, lg, lb)
        xp = _ffn(alpha, xpb, xp, wg, wu, wd, lg, lb)

    new_pool_s = jnp.concatenate([state_pool[:, :, steps:], jnp.stack(sample_pool_rows)], axis=2)
    return (xp.reshape(bp, seq, d), rows_to_sample(xs), jnp.stack(new_wkv_p), jnp.stack(new_shift_p),
            jnp.stack(new_pool_p), jnp.stack(new_wkv_s), jnp.stack(new_shift_s), new_pool_s)
```

```python
import functools
import math

import jax
import jax.numpy as jnp
from jax import lax
from jax.experimental import pallas as pl
from jax.experimental.pallas import tpu as pltpu

F32 = jnp.float32
BF16 = jnp.bfloat16

HEAD = 64
PAIR = 2 * HEAD
CHUNK = 64
SEQ_PER_CHUNK = 16
GN_EPS = 64e-5
LN_EPS = 1e-5
POOL_WINDOWS = (2, 4, 8, 16)
POOL_BUF = 15
HALO = 16
SHIFT_HALO = 8
LORA_PAD = 128

TM = 512
TM_MIX = 256
TM_FFN = 1024
TM_RKV = 1024
TN = 512
TN_CAST = 256
TF = 512
TF_CAST = 256
FFN_DOWN_CHUNK = 512
WKV_ROWS = 256
WKV_PAIRS = 16
WKV_SAMPLE_PAIRS = 4
VMEM_LIMIT = 56 * 1024 * 1024


def _cparams(*sem):
    return pltpu.CompilerParams(dimension_semantics=sem, vmem_limit_bytes=VMEM_LIMIT)


def _mm(a, b):
    return jnp.dot(a, b, preferred_element_type=F32)


def _mm_nt(a, b):
    return lax.dot_general(a, b, (((1,), (1,)), ((), ())), preferred_element_type=F32)


def _mm_tn(a, b):
    return lax.dot_general(a, b, (((0,), (0,)), ((), ())), preferred_element_type=F32)


def _sigmoid(x):
    return 1.0 / (1.0 + jnp.exp(-x))


def _layer_norm(v, g, b):
    mu = jnp.mean(v, axis=-1, keepdims=True)
    c = v - mu
    var = jnp.mean(c * c, axis=-1, keepdims=True)
    return c * lax.rsqrt(var + LN_EPS) * g + b


def _split2(x):
    hi = x.astype(BF16)
    return hi, (x - hi.astype(F32)).astype(BF16)


def _row(a):
    return pl.BlockSpec((1, a.shape[-1]), lambda *_: (0, 0))


def _shifted_rows(x, halo_ref, tile_in_seq):
    before = jnp.where(tile_in_seq == 0, 0.0, halo_ref[SHIFT_HALO - 1:SHIFT_HALO, :])
    first_row = lax.broadcasted_iota(jnp.int32, x.shape, 0) == 0
    return jnp.where(first_row, before, pltpu.roll(x, 1, 0))


def _shift_halo_spec(rows, d, tile_index):
    return pl.BlockSpec((SHIFT_HALO, d), lambda *g: (jnp.maximum(tile_index(*g) * (rows // SHIFT_HALO) - 1, 0), 0))


def _mix_kernel(has_vres, tiles_per_seq, *refs):
    refs = list(refs)
    x_ref, prev_ref, mu_ref, w0_ref, a0_ref = refs[:5]
    del refs[:5]
    v0_ref = refs.pop(0) if has_vres else None
    w1_ref, w2_ref, a1_ref, a2_ref = refs[:4]
    del refs[:4]
    v1_ref, v2_ref = (refs.pop(0), refs.pop(0)) if has_vres else (None, None)
    g1_ref, g2_ref = refs[:2]
    del refs[:2]
    if tiles_per_seq is None:
        xr_ref, xk_ref, xv_ref = refs[:3]
        del refs[:3]
    lw_ref, ag_ref, g_ref = refs[:3]
    vg_ref = refs[3] if has_vres else None

    x = x_ref[...]
    if tiles_per_seq is None:
        xprev = prev_ref[...]
    else:
        xprev = _shifted_rows(x, prev_ref, pl.program_id(0) % tiles_per_seq)
    xx = xprev - x

    def mix(j):
        return (x + xx * mu_ref[j:j + 1, :]).astype(BF16)

    if tiles_per_seq is None:
        xr_ref[...] = mix(0)
        xk_ref[...] = mix(2)
        xv_ref[...] = mix(3)

    u = w0_ref[...] + _mm(jnp.tanh(_mm(mix(1), w1_ref[...])).astype(BF16), w2_ref[...])
    z = -u
    softplus = jnp.maximum(z, 0.0) + jnp.log(1.0 + jnp.exp(-jnp.abs(z)))
    lw_ref[...] = -jnp.exp(-softplus - 0.5)
    ag_ref[...] = _sigmoid(a0_ref[...] + _mm(_mm(mix(4), a1_ref[...]).astype(BF16), a2_ref[...])).astype(BF16)
    g_ref[...] = _mm(_sigmoid(_mm(mix(5), g1_ref[...])).astype(BF16), g2_ref[...]).astype(BF16)
    if has_vres:
        vg_ref[...] = _sigmoid(
            v0_ref[...] + _mm(_mm(mix(3), v1_ref[...]).astype(BF16), v2_ref[...])).astype(BF16)


def _rwkv_mix(x, xprev, seq, params):
    m, d = x.shape
    has_vres = len(params) == 12
    tile = pl.BlockSpec((TM_MIX, d), lambda i: (i, 0))
    if xprev is None:
        prev, prev_spec, tiles_per_seq = x, _shift_halo_spec(TM_MIX, d, lambda i: i), seq // TM_MIX
        out_shape = []
    else:
        prev, prev_spec, tiles_per_seq = xprev, tile, None
        out_shape = [jax.ShapeDtypeStruct((m, d), BF16)] * 3

    def full(a):
        return pl.BlockSpec(a.shape, lambda i: (0,) * a.ndim)

    out_shape += [jax.ShapeDtypeStruct((m, d), F32)] + [jax.ShapeDtypeStruct((m, d), BF16)] * (3 if has_vres else 2)
    return pl.pallas_call(
        functools.partial(_mix_kernel, has_vres, tiles_per_seq),
        grid=(m // TM_MIX,),
        in_specs=[tile, prev_spec] + [full(a) for a in params],
        out_specs=[tile] * len(out_shape),
        out_shape=out_shape,
        compiler_params=_cparams("parallel"),
        name="rwkv_mix",
    )(x, prev, *params)


def _rkv_kernel(tiles_per_seq, x_ref, halo_ref, mu_ref, wr_ref, wk_ref, wv_ref, r_ref, k_ref, v_ref):
    x = x_ref[...]
    xx = _shifted_rows(x, halo_ref, pl.program_id(1) % tiles_per_seq) - x
    for j, w_ref, o_ref in ((0, wr_ref, r_ref), (2, wk_ref, k_ref), (3, wv_ref, v_ref)):
        o_ref[...] = _mm((x + xx * mu_ref[j:j + 1, :]).astype(BF16), w_ref[...]).astype(BF16)


def _rkv_proj(x, mu, wr, wk, wv, seq):
    m, d = x.shape
    xs = pl.BlockSpec((TM_RKV, d), lambda n, i: (i, 0))
    ws = pl.BlockSpec((d, TN), lambda n, i: (0, n))
    os = pl.BlockSpec((TM_RKV, TN), lambda n, i: (i, n))
    return pl.pallas_call(
        functools.partial(_rkv_kernel, seq // TM_RKV),
        grid=(d // TN, m // TM_RKV),
        in_specs=[xs, _shift_halo_spec(TM_RKV, d, lambda n, i: i),
                  pl.BlockSpec(mu.shape, lambda n, i: (0, 0)), ws, ws, ws],
        out_specs=[os, os, os],
        out_shape=[jax.ShapeDtypeStruct((m, d), BF16)] * 3,
        compiler_params=_cparams("parallel", "arbitrary"),
        name="rkv_proj",
    )(x, x, mu, wr, wk, wv)


def _rkv_cast_kernel(xr_ref, xk_ref, xv_ref, wr_ref, wk_ref, wv_ref, r_ref, k_ref, v_ref, wrb_ref, wkb_ref, wvb_ref):
    for x_ref, w_ref, o_ref, wb_ref in ((xr_ref, wr_ref, r_ref, wrb_ref), (xk_ref, wk_ref, k_ref, wkb_ref),
                                        (xv_ref, wv_ref, v_ref, wvb_ref)):
        wb = w_ref[...].astype(BF16)
        wb_ref[...] = wb
        o_ref[...] = _mm(x_ref[...], wb).astype(BF16)


def _rkv_proj_cast(xr, xk, xv, wr, wk, wv, layer):
    m, d = xr.shape
    assert m == TM
    xs = pl.BlockSpec((TM, d), lambda n: (0, 0))
    ws = pl.BlockSpec((None, d, TN_CAST), lambda n: (layer, 0, n))
    os = pl.BlockSpec((TM, TN_CAST), lambda n: (0, n))
    wbs = pl.BlockSpec((d, TN_CAST), lambda n: (0, n))
    return pl.pallas_call(
        _rkv_cast_kernel,
        grid=(d // TN_CAST,),
        in_specs=[xs, xs, xs, ws, ws, ws],
        out_specs=[os, os, os, wbs, wbs, wbs],
        out_shape=[jax.ShapeDtypeStruct((m, d), BF16)] * 3 + [jax.ShapeDtypeStruct((d, d), BF16)] * 3,
        compiler_params=_cparams("parallel"),
        name="rkv_proj_cast",
    )(xr, xk, xv, wr, wk, wv)


class _ChunkMasks:
    def __init__(self, sample):
        row = lax.broadcasted_iota(jnp.int32, (CHUNK, PAIR), 0)
        lane = lax.broadcasted_iota(jnp.int32, (CHUNK, PAIR), 1)
        col = lane & (CHUNK - 1)
        r2 = lax.broadcasted_iota(jnp.int32, (CHUNK, CHUNK), 0)
        c2 = lax.broadcasted_iota(jnp.int32, (CHUNK, CHUNK), 1)
        if sample:
            shift = int(math.log2(SEQ_PER_CHUNK))
            seq_mask = SEQ_PER_CHUNK - 1
            same = (row & seq_mask) == (col & seq_mask)
            self.strict = same & ((col >> shift) < (row >> shift))
            self.incl = same & ((col >> shift) <= (row >> shift))
            tri = ((r2 & seq_mask) == (c2 & seq_mask)) & ((c2 >> shift) <= (r2 >> shift))
            self.squarings = 1
        else:
            self.strict = col < row
            self.incl = col <= row
            tri = c2 <= r2
            self.squarings = 5
        self.sample = sample
        self.eye = jnp.where(row == col, 1.0, 0.0).astype(F32)
        self.tri = jnp.where(tri, 1.0, 0.0).astype(BF16)
        self.head0 = lane < HEAD
        r128 = lax.broadcasted_iota(jnp.int32, (PAIR, PAIR), 0)
        c128 = lax.broadcasted_iota(jnp.int32, (PAIR, PAIR), 1)
        self.same_head = (r128 < HEAD) == (c128 < HEAD)
        self.diag128 = r128 == c128


def _blk(y, cm):
    return jnp.concatenate([jnp.where(cm.head0, y, 0.0), jnp.where(cm.head0, 0.0, y)], axis=0).astype(BF16)


def _headsum(xs, cm):
    parts = [(jnp.sum(jnp.where(cm.head0, x, 0.0), axis=1, keepdims=True),
              jnp.sum(jnp.where(cm.head0, 0.0, x), axis=1, keepdims=True)) for x in xs]
    return [jnp.where(cm.head0, s0, s1) for s0, s1 in parts]


def _wkv_pre(kraw, v, ag, vf, vg, kkw, ka, cm):
    kkv = [k * w for k, w in zip(kraw, kkw)]
    sumsq = _headsum([x * x for x in kkv], cm)
    kkn = [x / jnp.maximum(jnp.sqrt(s), 1e-12) for x, s in zip(kkv, sumsq)]
    kmod = [k * (1.0 + (g - 1.0) * c) for k, g, c in zip(kraw, ag, ka)]
    if vf is not None:
        v = [x + (f - x) * g for x, f, g in zip(v, vf, vg)]
    return kmod, v, [-x for x in kkn], [x * g for x, g in zip(kkn, ag)]


def _wkv_post(y, r, kmod, v, g, rk, gng, gnb, cm):
    inv_n = 1.0 / HEAD
    n = len(y)
    sums = _headsum(list(y) + [a * b * c for a, b, c in zip(r, kmod, rk)], cm)
    yc = [a - s * inv_n for a, s in zip(y, sums[:n])]
    var = _headsum([c * c for c in yc], cm)
    return [((c * lax.rsqrt(s * inv_n + GN_EPS) * gg + gb + bs * vv) * gt).astype(BF16)
            for c, s, gg, gb, bs, vv, gt in zip(yc, var, gng, gnb, sums[n:], v, g)]


def _chunk_prep(r, lw, k, v, a, b, cm):
    splits = [_split2(x) for x in lw]
    cum = [_mm(cm.tri, hi) + _mm(cm.tri, lo) for hi, lo in splits]
    if cm.sample:
        cl = [jnp.concatenate([c[CHUNK - SEQ_PER_CHUNK:, :]] * (CHUNK // SEQ_PER_CHUNK), axis=0) for c in cum]
    else:
        cl = [jnp.broadcast_to(c[CHUNK - 1:CHUNK, :], c.shape) for c in cum]
    w_inv = [jnp.exp(-c) for c in cum]
    w_last = [jnp.exp(c) for c in cl]
    rt = [x * jnp.exp(c) for x, c in zip(r, cum)]
    at = [x * jnp.exp(c - l) for x, c, l in zip(a, cum, lw)]
    bt = [x * w for x, w in zip(b, w_inv)]
    kt = [x * w for x, w in zip(k, w_inv)]
    x = [jnp.concatenate([p, q], axis=0).astype(BF16) for p, q in zip(at, rt)]
    blk_bk = [jnp.concatenate([_blk(p, cm), _blk(q, cm)], axis=0) for p, q in zip(bt, kt)]
    prod = [_mm_nt(p, q) for p, q in zip(x, blk_bk)]
    nak = [jnp.where(cm.strict, y[:CHUNK, PAIR:], 0.0).astype(BF16) for y in prod]
    mrbk = [jnp.concatenate([jnp.where(cm.incl, y[CHUNK:, :PAIR], 0.0),
                             jnp.where(cm.incl, y[CHUNK:, PAIR:], 0.0)], axis=1).astype(BF16) for y in prod]
    p = [jnp.where(cm.strict, y[:CHUNK, :PAIR], 0.0) for y in prod]
    t = [cm.eye + y for y in p]
    p = [_mm(y.astype(BF16), _blk(y, cm)) for y in p]
    for _ in range(cm.squarings - 1):
        both = [_mm(jnp.concatenate([y, z], axis=0).astype(BF16), _blk(y, cm)) for y, z in zip(p, t)]
        p = [y[:CHUNK] for y in both]
        t = [z + y[CHUNK:] for z, y in zip(t, both)]
    t = [z + _mm(z.astype(BF16), _blk(y, cm)) for z, y in zip(t, p)]
    tb = [y.astype(BF16) for y in t]
    blk_v = [_blk(y, cm) for y in v]
    q = [_mm(m, w) for m, w in zip(nak, blk_v)]
    au = [_mm(m, jnp.concatenate([_blk(y, cm), _blk(z, cm)], axis=1)) for m, y, z in zip(tb, at, q)]
    ah = [y[:, :PAIR] for y in au]
    uh = [y[:, PAIR:] for y in au]
    rh = [y + _mm(m[:, :PAIR], _blk(z, cm)) for y, m, z in zip(rt, mrbk, ah)]
    yh = [_mm(m, jnp.concatenate([_blk(z, cm), w], axis=0)) for m, z, w in zip(mrbk, uh, blk_v)]
    bh = [y * w for y, w in zip(bt, w_last)]
    kh = [y * w for y, w in zip(kt, w_last)]
    return ah, uh, rh, yh, bh, kh, w_last


def _wkv_prompt_kernel(has_vres, n_chunks, n_pairs, *refs):
    if has_vres:
        (r_ref, k_ref, v_ref, lw_ref, ag_ref, g_ref, vf_ref, vg_ref,
         kkw_ref, ka_ref, rk_ref, gng_ref, gnb_ref, z_ref, s_ref, s_scr) = refs
    else:
        (r_ref, k_ref, v_ref, lw_ref, ag_ref, g_ref,
         kkw_ref, ka_ref, rk_ref, gng_ref, gnb_ref, z_ref, s_ref, s_scr) = refs
    step = pl.program_id(1)

    @pl.when(step == 0)
    def _():
        s_scr[...] = jnp.zeros_like(s_scr)

    cm = _ChunkMasks(sample=False)
    lanes = [slice(p * PAIR, (p + 1) * PAIR) for p in range(n_pairs)]

    def chunk(c, carry):
        rows = pl.ds(pl.multiple_of(c * CHUNK, CHUNK), CHUNK)

        def tok(ref):
            return [ref[rows, l].astype(F32) for l in lanes]

        def chan(ref):
            return [ref[:, l] for l in lanes]

        r = tok(r_ref)
        kmod, v, a, b = _wkv_pre(tok(k_ref), tok(v_ref), tok(ag_ref),
                                 tok(vf_ref) if has_vres else None, tok(vg_ref) if has_vres else None,
                                 chan(kkw_ref), chan(ka_ref), cm)
        ah, uh, rh, yh, bh, kh, w_last = _chunk_prep(r, tok(lw_ref), kmod, v, a, b, cm)
        s16 = [s_scr[p].astype(BF16) for p in range(n_pairs)]
        y = [_mm_nt(x.astype(BF16), h) + z for x, h, z in zip(rh, s16, yh)]
        bh16 = [x.astype(BF16) for x in bh]
        ab = [_mm_tn(x.astype(BF16), w) for x, w in zip(ah, bh16)]
        uvbk = [_mm_tn(jnp.concatenate([p, q], axis=0).astype(BF16),
                       jnp.concatenate([w, x.astype(BF16)], axis=0))
                for p, q, w, x in zip(uh, v, bh16, kh)]
        phi = [(jnp.where(cm.diag128, jnp.broadcast_to(w[:1, :], (PAIR, PAIR)), 0.0)
                + jnp.where(cm.same_head, m, 0.0)).astype(BF16) for w, m in zip(w_last, ab)]
        s_new = [_mm(h, f) + jnp.where(cm.same_head, m, 0.0) for h, f, m in zip(s16, phi, uvbk)]
        for p in range(n_pairs):
            s_scr[p] = s_new[p]
        z = _wkv_post(y, r, kmod, v, tok(g_ref), chan(rk_ref), chan(gng_ref), chan(gnb_ref), cm)
        for l, zz in zip(lanes, z):
            z_ref[rows, l] = zz
        return carry

    lax.fori_loop(0, n_chunks, chunk, 0)

    @pl.when(step == pl.num_programs(1) - 1)
    def _():
        for p in range(n_pairs):
            s = s_scr[p]
            s_ref[0, 2 * p] = s[:HEAD, :HEAD]
            s_ref[0, 2 * p + 1] = s[HEAD:, HEAD:]


def _wkv_prompt(tok, vres, chan, batch, seq):
    m, d = tok[0].shape
    has_vres = vres is not None
    assert d == WKV_PAIRS * PAIR
    steps = seq // WKV_ROWS
    tile = pl.BlockSpec((WKV_ROWS, d), lambda b, s: (b * steps + s, 0))
    args = tuple(tok) + (tuple(vres) if has_vres else ()) + tuple(chan)
    n_tok = len(tok) + (2 if has_vres else 0)
    n_heads = d // HEAD
    return pl.pallas_call(
        functools.partial(_wkv_prompt_kernel, has_vres, WKV_ROWS // CHUNK, WKV_PAIRS),
        grid=(batch, steps),
        in_specs=[tile] * n_tok + [_row(c) for c in chan],
        out_specs=[tile, pl.BlockSpec((1, n_heads, HEAD, HEAD), lambda b, s: (b, 0, 0, 0))],
        out_shape=[jax.ShapeDtypeStruct((m, d), BF16),
                   jax.ShapeDtypeStruct((batch, n_heads, HEAD, HEAD), F32)],
        scratch_shapes=[pltpu.VMEM((WKV_PAIRS, PAIR, PAIR), F32)],
        compiler_params=_cparams("parallel", "arbitrary"),
        name="wkv_prompt",
    )(*args)


def _wkv_sample_kernel(has_vres, n_pairs, *refs):
    if has_vres:
        (r_ref, k_ref, v_ref, lw_ref, ag_ref, g_ref, vf_ref, vg_ref,
         kkw_ref, ka_ref, rk_ref, gng_ref, gnb_ref, s_in_ref, z_ref, s_out_ref,
         x_scr, u_scr, bk_scr, gy_scr) = refs
    else:
        (r_ref, k_ref, v_ref, lw_ref, ag_ref, g_ref,
         kkw_ref, ka_ref, rk_ref, gng_ref, gnb_ref, s_in_ref, z_ref, s_out_ref,
         x_scr, u_scr, bk_scr, gy_scr) = refs
    cm = _ChunkMasks(sample=True)
    row8 = lax.broadcasted_iota(jnp.int32, (8, PAIR), 0)
    zeros_h = jnp.zeros((HEAD, HEAD), F32)
    steps = CHUNK // SEQ_PER_CHUNK
    lanes = [slice(p * PAIR, (p + 1) * PAIR) for p in range(n_pairs)]

    def tok(ref):
        return [ref[:, l].astype(F32) for l in lanes]

    r = tok(r_ref)
    kmod, v, a, b = _wkv_pre(tok(k_ref), tok(v_ref), tok(ag_ref),
                             tok(vf_ref) if has_vres else None, tok(vg_ref) if has_vres else None,
                             tok(kkw_ref), tok(ka_ref), cm)
    ah, uh, rh, yh, bh, kh, w_last = _chunk_prep(r, tok(lw_ref), kmod, v, a, b, cm)
    for p in range(n_pairs):
        x_scr[p, 0:CHUNK, :] = ah[p]
        x_scr[p, CHUNK:, :] = rh[p]
        u_scr[p, 0:CHUNK, :] = uh[p]
        u_scr[p, CHUNK:, :] = yh[p]
        bk_scr[p, 0:CHUNK, :] = bh[p]
        bk_scr[p, CHUNK:, :] = kh[p]
        gy_scr[p, CHUNK:, :] = v[p]

    seqs = range(SEQ_PER_CHUNK)
    for p in range(n_pairs):
        pick = [pl.ds(i, 2 * steps, stride=SEQ_PER_CHUNK) for i in seqs]
        s = [jnp.concatenate([jnp.concatenate([s_in_ref[i, 2 * p], zeros_h], axis=1),
                              jnp.concatenate([zeros_h, s_in_ref[i, 2 * p + 1]], axis=1)], axis=0) for i in seqs]
        uy = [_mm_nt(x_scr[p, pk, :].astype(BF16), m.astype(BF16)) + u_scr[p, pk, :] for pk, m in zip(pick, s)]
        uv = [jnp.where(row8 < steps, m, gy_scr[p, pk, :]) for pk, m in zip(pick, uy)]
        for pk, m in zip(pick, uy):
            gy_scr[p, pk, :] = m
        upd = [_mm_tn(m.astype(BF16), bk_scr[p, pk, :].astype(BF16)) for pk, m in zip(pick, uv)]
        for i in seqs:
            s_new = s[i] * w_last[p][i:i + 1, :] + jnp.where(cm.same_head, upd[i], 0.0)
            s_out_ref[i, 2 * p] = s_new[:HEAD, :HEAD]
            s_out_ref[i, 2 * p + 1] = s_new[HEAD:, HEAD:]

    y = [gy_scr[p, CHUNK:, :] for p in range(n_pairs)]
    z = _wkv_post(y, r, kmod, v, tok(g_ref), tok(rk_ref), tok(gng_ref), tok(gnb_ref), cm)
    for l, zz in zip(lanes, z):
        z_ref[:, l] = zz


def _wkv_sample(tok, vres, chan, state):
    m, d = tok[0].shape
    nb = state.shape[0]
    has_vres = vres is not None
    n_pairs = WKV_SAMPLE_PAIRS
    width = n_pairs * PAIR
    tile = pl.BlockSpec((CHUNK, width), lambda i, p: (i, p))
    row = pl.BlockSpec((1, width), lambda i, p: (0, p))
    blk = (SEQ_PER_CHUNK, 2 * n_pairs, HEAD, HEAD)
    args = tuple(tok) + (tuple(vres) if has_vres else ()) + tuple(chan) + (state,)
    n_tok = len(tok) + (2 if has_vres else 0)
    return pl.pallas_call(
        functools.partial(_wkv_sample_kernel, has_vres, n_pairs),
        grid=(nb // SEQ_PER_CHUNK, d // width),
        in_specs=[tile] * n_tok + [row] * len(chan) + [pl.BlockSpec(blk, lambda i, p: (i, p, 0, 0))],
        out_specs=[tile, pl.BlockSpec(blk, lambda i, p: (i, p, 0, 0))],
        out_shape=[jax.ShapeDtypeStruct((m, d), BF16), jax.ShapeDtypeStruct(state.shape, F32)],
        scratch_shapes=[pltpu.VMEM((n_pairs, 2 * CHUNK, PAIR), F32)] * 4,
        compiler_params=_cparams("parallel", "parallel"),
        name="wkv_sample",
    )(*args)


def _out_ln_kernel(alpha, z_ref, x_ref, w_ref, g_ref, b_ref, o_ref, ob_ref):
    y = _layer_norm(alpha * x_ref[...] + _mm(z_ref[...], w_ref[...]), g_ref[...], b_ref[...])
    o_ref[...] = y
    ob_ref[...] = y.astype(BF16)


def _out_ln(alpha, z, x, w, layer, g, b):
    m, d = x.shape
    tile = pl.BlockSpec((TM, d), lambda i: (i, 0))
    return pl.pallas_call(
        functools.partial(_out_ln_kernel, alpha),
        grid=(m // TM,),
        in_specs=[tile, tile, pl.BlockSpec((None, d, d), lambda i: (layer, 0, 0)), _row(g), _row(b)],
        out_specs=[tile, tile],
        out_shape=[jax.ShapeDtypeStruct((m, d), F32), jax.ShapeDtypeStruct((m, d), BF16)],
        compiler_params=_cparams("parallel"),
        name="out_ln",
    )(z, x, w, g, b)


def _ffn_kernel(alpha, emit_weights, *refs):
    if emit_weights:
        (xb_ref, x_hbm, wg_ref, wu_ref, wd_ref, g_ref, b_ref,
         o_ref, wgb_ref, wub_ref, wdb_ref, xres_ref, sem) = refs
    else:
        xb_ref, x_hbm, wg_ref, wu_ref, wd_ref, g_ref, b_ref, o_ref, xres_ref, sem = refs
    f = pl.program_id(1)
    rows = o_ref.shape[0]

    def residual_copy():
        start = pl.multiple_of(pl.program_id(0) * rows, rows)
        return pltpu.make_async_copy(x_hbm.at[pl.ds(start, rows), :], xres_ref, sem)

    @pl.when(f == 0)
    def _():
        residual_copy().start()
        o_ref[...] = jnp.zeros_like(o_ref)

    if emit_weights:
        wgb_ref[...] = wg_ref[...].astype(BF16)
        wub_ref[...] = wu_ref[...].astype(BF16)
        wdb_ref[...] = wd_ref[...].astype(BF16)
        wg_ref, wu_ref, wd_ref = wgb_ref, wub_ref, wdb_ref
    xb = xb_ref[...]
    gate = _mm(xb, wg_ref[...])
    up = _mm(xb, wu_ref[...])
    act = (gate * _sigmoid(gate) * up).astype(BF16)
    for c in range(0, o_ref.shape[1], FFN_DOWN_CHUNK):
        cols = slice(c, c + FFN_DOWN_CHUNK)
        o_ref[:, cols] += _mm(act, wd_ref[:, cols])

    @pl.when(f == pl.num_programs(1) - 1)
    def _():
        residual_copy().wait()
        o_ref[...] = _layer_norm(alpha * xres_ref[...] + o_ref[...], g_ref[...], b_ref[...])


def _ffn(alpha, xb, x, wg, wu, wd, g, b):
    m, d = x.shape
    d_ff = wd.shape[0]
    tile = pl.BlockSpec((TM_FFN, d), lambda i, f: (i, 0))
    in_tile = pl.BlockSpec((TM_FFN, d), lambda i, f: (i, 0), pipeline_mode=pl.Buffered(1))
    return pl.pallas_call(
        functools.partial(_ffn_kernel, alpha, False),
        grid=(m // TM_FFN, d_ff // TF),
        in_specs=[in_tile, pl.BlockSpec(memory_space=pl.ANY),
                  pl.BlockSpec((d, TF), lambda i, f: (0, f)),
                  pl.BlockSpec((d, TF), lambda i, f: (0, f)),
                  pl.BlockSpec((TF, d), lambda i, f: (f, 0)),
                  _row(g), _row(b)],
        out_specs=tile,
        out_shape=jax.ShapeDtypeStruct((m, d), F32),
        scratch_shapes=[pltpu.VMEM((TM_FFN, d), F32), pltpu.SemaphoreType.DMA(())],
        compiler_params=_cparams("parallel", "arbitrary"),
        name="ffn",
    )(xb, x, wg, wu, wd, g, b)


def _ffn_cast(alpha, xb, x, w_in, w_down, layer, g, b):
    m, d = x.shape
    assert m == TM
    d_ff = w_down.shape[1]
    nf = d_ff // TF_CAST
    tile = pl.BlockSpec((TM, d), lambda i, f: (i, 0))
    return pl.pallas_call(
        functools.partial(_ffn_kernel, alpha, True),
        grid=(1, nf),
        in_specs=[tile, pl.BlockSpec(memory_space=pl.ANY),
                  pl.BlockSpec((None, d, TF_CAST), lambda i, f: (layer, 0, f)),
                  pl.BlockSpec((None, d, TF_CAST), lambda i, f: (layer, 0, f + nf)),
                  pl.BlockSpec((None, TF_CAST, d), lambda i, f: (layer, f, 0)),
                  _row(g), _row(b)],
        out_specs=[tile,
                   pl.BlockSpec((d, TF_CAST), lambda i, f: (0, f)),
                   pl.BlockSpec((d, TF_CAST), lambda i, f: (0, f)),
                   pl.BlockSpec((TF_CAST, d), lambda i, f: (f, 0))],
        out_shape=[jax.ShapeDtypeStruct((m, d), F32),
                   jax.ShapeDtypeStruct((d, d_ff), BF16), jax.ShapeDtypeStruct((d, d_ff), BF16),
                   jax.ShapeDtypeStruct((d_ff, d), BF16)],
        scratch_shapes=[pltpu.VMEM((TM, d), F32), pltpu.SemaphoreType.DMA(())],
        compiler_params=_cparams("arbitrary", "arbitrary"),
        name="ffn_cast",
    )(xb, x, w_in, w_in, w_down, g, b)


def _pool_prompt_kernel(alpha, tiles_per_seq, x_ref, halo_ref, w_ref, sc_ref, g_ref, b_ref, o_ref, ob_ref):
    tile_in_seq = pl.program_id(0) % tiles_per_seq
    x = x_ref[...]
    halo = jnp.where(tile_in_seq == 0, 0.0, halo_ref[...])
    gw = x.shape[1] // len(POOL_WINDOWS)
    pos = tile_in_seq * TM + lax.broadcasted_iota(jnp.int32, (TM, gw), 0)
    outs = []
    for gi, win in enumerate(POOL_WINDOWS):
        lanes = slice(gi * gw, (gi + 1) * gw)
        xg = x[:, lanes]
        s = jnp.concatenate([halo[:, lanes], xg], axis=0)
        span = 1
        while span < win:
            s = s[span:] + s[:-span]
            span *= 2
        first = HALO - (win - 1)
        cnt = jnp.minimum(win, pos + 1).astype(F32)
        dg = s[first:first + TM] / cnt - xg
        outs.append(_mm(dg.astype(BF16), w_ref[gi]))
    h = jnp.concatenate(outs, axis=1) * sc_ref[...]
    y = _layer_norm(alpha * x + h, g_ref[...], b_ref[...])
    o_ref[...] = y
    ob_ref[...] = y.astype(BF16)


def _pool_prompt(alpha, x, w, layer, scale, g, b, seq):
    m, d = x.shape
    tile = pl.BlockSpec((TM, d), lambda i: (i, 0))
    halo = pl.BlockSpec((HALO, d), lambda i: (jnp.maximum(i * (TM // HALO) - 1, 0), 0))
    return pl.pallas_call(
        functools.partial(_pool_prompt_kernel, alpha, seq // TM),
        grid=(m // TM,),
        in_specs=[tile, halo, pl.BlockSpec((None,) + w.shape[1:], lambda i: (layer, 0, 0, 0)),
                  _row(scale), _row(g), _row(b)],
        out_specs=[tile, tile],
        out_shape=[jax.ShapeDtypeStruct((m, d), F32), jax.ShapeDtypeStruct((m, d), BF16)],
        compiler_params=_cparams("parallel"),
        name="pool_prompt",
    )(x, x, w, scale, g, b)


def _pool_sample_kernel(ext_ref, w_ref, sc_ref, h_ref):
    steps = ext_ref.shape[0] - POOL_BUF
    gi = pl.program_id(0)
    ds = []
    for t in range(steps):
        cur = ext_ref[POOL_BUF + t]
        acc16 = cur
        sums = {}
        for i in range(1, max(POOL_WINDOWS)):
            acc16 = acc16 + ext_ref[POOL_BUF + t - i]
            if i + 1 in POOL_WINDOWS:
                sums[i + 1] = acc16
        d = sums[POOL_WINDOWS[-1]] * (1.0 / POOL_WINDOWS[-1])
        for j, win in enumerate(POOL_WINDOWS[:-1]):
            d = jnp.where(gi == j, sums[win] * (1.0 / win), d)
        ds.append((d - cur).astype(BF16))
    h_ref[...] = _mm(jnp.concatenate(ds, axis=0), w_ref[...]) * sc_ref[...]


def _pool_sample(ext, w, layer, scale):
    n, nb, d = ext.shape
    ng = len(POOL_WINDOWS)
    gw = d // ng
    return pl.pallas_call(
        _pool_sample_kernel,
        grid=(ng,),
        in_specs=[pl.BlockSpec((n, nb, gw), lambda gi: (0, 0, gi)),
                  pl.BlockSpec((None, None, gw, gw), lambda gi: (layer, gi, 0, 0)),
                  pl.BlockSpec((1, gw), lambda gi: (0, gi))],
        out_specs=pl.BlockSpec(((n - POOL_BUF) * nb, gw), lambda gi: (0, gi)),
        out_shape=jax.ShapeDtypeStruct(((n - POOL_BUF) * nb, d), F32),
        compiler_params=_cparams("parallel"),
        name="pool_sample",
    )(ext, w, scale)


def _add_ln_kernel(alpha, x_ref, h_ref, g_ref, b_ref, o_ref, ob_ref):
    y = _layer_norm(alpha * x_ref[...] + h_ref[...], g_ref[...], b_ref[...])
    o_ref[...] = y
    ob_ref[...] = y.astype(BF16)


def _add_ln(alpha, x, h, g, b):
    m, d = x.shape
    tile = pl.BlockSpec((TM, d), lambda i: (i, 0))
    return pl.pallas_call(
        functools.partial(_add_ln_kernel, alpha),
        grid=(m // TM,),
        in_specs=[tile, tile, _row(g), _row(b)],
        out_specs=[tile, tile],
        out_shape=[jax.ShapeDtypeStruct((m, d), F32), jax.ShapeDtypeStruct((m, d), BF16)],
        compiler_params=_cparams("parallel"),
        name="add_ln",
    )(x, h, g, b)


def _pad_lora(w_a, w_b):
    rank = w_a.shape[1]
    pad = (-rank) % LORA_PAD
    return (jnp.pad(w_a, ((0, 0), (0, pad))).astype(BF16), jnp.pad(w_b, ((0, pad), (0, 0))).astype(BF16))


def kernel(x_prompt, x_sample, state_wkv, state_shift, state_pool, ln_g, ln_b, rw_mu, rw_wr, rw_wk, rw_wv, rw_wo, rw_w0, rw_w1, rw_w2, rw_a0, rw_a1, rw_a2, rw_v0, rw_v1, rw_v2, rw_g1, rw_g2, rw_kk, rw_ka, rw_rk, rw_gn_g, rw_gn_b, pool_w, pool_scale, ffn_w_in, ffn_w_down):
    bp, seq, d = x_prompt.shape
    bs, steps, _ = x_sample.shape
    depth = ln_g.shape[0]
    n_mixers = 2
    alpha = float((2 * depth) ** 0.25)
    m_sample = bs * steps
    nblk = bs // SEQ_PER_CHUNK
    assert steps * SEQ_PER_CHUNK == CHUNK and bs % SEQ_PER_CHUNK == 0
    assert seq % WKV_ROWS == 0 and seq % TM == 0 and m_sample % TM == 0

    def sample_to_rows(a):
        return a.reshape(nblk, SEQ_PER_CHUNK, steps, d).transpose(0, 2, 1, 3).reshape(m_sample, d)

    def rows_to_sample(a):
        return a.reshape(nblk, steps, SEQ_PER_CHUNK, d).transpose(0, 2, 1, 3).reshape(bs, steps, d)

    wo, w_pool = rw_wo.astype(BF16), pool_w.astype(BF16)

    xp = x_prompt.reshape(bp * seq, d)
    xs = sample_to_rows(x_sample)
    xpb = xsb = None
    vf_p = vf_s = None
    new_wkv_p, new_wkv_s, new_shift_p, new_shift_s, new_pool_p, sample_pool_rows = [], [], [], [], [], []
    for i in range(depth):
        j = i // n_mixers
        lg, lb = ln_g[i, 0][None, :], ln_b[i, 0][None, :]
        xs4 = xs.reshape(nblk, steps, SEQ_PER_CHUNK, d)
        if i % n_mixers == 0:
            prev_s = jnp.concatenate([state_shift[j].reshape(nblk, 1, SEQ_PER_CHUNK, d), xs4[:, :-1]],
                                     axis=1).reshape(m_sample, d)
            new_shift_p.append(jnp.concatenate([xp[(b + 1) * seq - 1:(b + 1) * seq] for b in range(bp)]))
            new_shift_s.append(xs4[:, -1].reshape(bs, d))

            w1, w2 = _pad_lora(rw_w1[j], rw_w2[j])
            a1, a2 = _pad_lora(rw_a1[j], rw_a2[j])
            g1, g2 = rw_g1[j].astype(BF16), rw_g2[j].astype(BF16)
            if j == 0:
                params = (rw_mu[j], rw_w0[j][None, :], rw_a0[j][None, :], w1, w2, a1, a2, g1, g2)
            else:
                v1, v2 = _pad_lora(rw_v1[j - 1], rw_v2[j - 1])
                params = (rw_mu[j], rw_w0[j][None, :], rw_a0[j][None, :], rw_v0[j - 1][None, :],
                          w1, w2, a1, a2, v1, v2, g1, g2)
            chan = (rw_kk[j][None, :], rw_ka[j][None, :], rw_rk[j].reshape(1, d),
                    rw_gn_g[j][None, :], rw_gn_b[j][None, :])

            mixed = _rwkv_mix(xs, prev_s, seq, params)
            r, k, v, wr, wk, wv = _rkv_proj_cast(mixed[0], mixed[1], mixed[2], rw_wr, rw_wk, rw_wv, j)
            tok_s, vg_s = (r, k, v) + tuple(mixed[3:6]), (mixed[6] if j > 0 else None)
            mixed = _rwkv_mix(xp, None, seq, params)
            r, k, v = _rkv_proj(xp, rw_mu[j], wr, wk, wv, seq)
            tok_p, vg_p = (r, k, v) + tuple(mixed[:3]), (mixed[3] if j > 0 else None)
            if j == 0:
                vf_p, vf_s = tok_p[2], tok_s[2]
            zp, s_p = _wkv_prompt(tok_p, None if j == 0 else (vf_p, vg_p), chan, bp, seq)
            zs, s_s = _wkv_sample(tok_s, None if j == 0 else (vf_s, vg_s), chan, state_wkv[j])
            new_wkv_p.append(s_p)
            new_wkv_s.append(s_s)
            xp, xpb = _out_ln(alpha, zp, xp, wo, j, lg, lb)
            xs, xsb = _out_ln(alpha, zs, xs, wo, j, lg, lb)
        else:
            xs_tb = xs4.transpose(1, 0, 2, 3).reshape(steps, bs, d)
            new_pool_p.append(jnp.stack([xp[(b + 1) * seq - POOL_BUF:(b + 1) * seq] for b in range(bp)]))
            sample_pool_rows.append(xs_tb.transpose(1, 0, 2))
            sc = pool_scale[j][None, :]
            ext = jnp.concatenate([state_pool[j].transpose(1, 0, 2), xs_tb], axis=0)
            h_tb = _pool_sample(ext, w_pool, j, sc)
            h_rows = h_tb.reshape(steps, nblk, SEQ_PER_CHUNK, d).transpose(1, 0, 2, 3).reshape(m_sample, d)
            xp, xpb = _pool_prompt(alpha, xp, w_pool, j, sc, lg, lb, seq)
            xs, xsb = _add_ln(alpha, xs, h_rows, lg, lb)
        lg, lb = ln_g[i, 1][None, :], ln_b[i, 1][None, :]
        xs, wg, wu, wd = _ffn_cast(alpha, xsb, xs, ffn_w_in, ffn_w_down, i, lg, lb)
        xp = _ffn(alpha, xpb, xp, wg, wu, wd, lg, lb)

    new_pool_s = jnp.concatenate([state_pool[:, :, steps:], jnp.stack(sample_pool_rows)], axis=2)
    return (xp.reshape(bp, seq, d), rows_to_sample(xs), jnp.stack(new_wkv_p), jnp.stack(new_shift_p),
            jnp.stack(new_pool_p), jnp.stack(new_wkv_s), jnp.stack(new_shift_s), new_pool_s)
```
